```python
import jax, jax.numpy as jnp
from jax import lax
import numpy as np

D_MODEL = 1024
BATCH = 4
SEQ = 8192
DEPTH = 1
DEC_BATCH = 8
DEC_SEQ = 64
PAST_LEN = 1024

CHUNK = 64
HEAD_DIM = 64
H_A = 8
H_B = 8
H_IDX = 8
D_IDX = 64
ROT_DIM = HEAD_DIM // 4
ROPE_THETA = 500000.0
TOPK_MAX = 256
BAND_CHUNKS = 8
BAND_ROWS = BAND_CHUNKS * CHUNK
REL_CLIP = 128
D_FF = 4 * D_MODEL
Q_BLOCK = 128
EPS = 1e-6
ATTN_SCALE = HEAD_DIM ** -0.5

W_A = H_A * HEAD_DIM
W_B = H_B * HEAD_DIM
SPLITS = [W_A, W_A, W_A, H_IDX * D_IDX, D_IDX, H_IDX, W_B, W_B, W_B, D_MODEL, D_MODEL]
SPLIT_POINTS = [sum(SPLITS[:i + 1]) for i in range(len(SPLITS) - 1)]
D_IN = sum(SPLITS)

kernel_name = 'hybrid_streaming_dsa_chunkband_step'


def rmsnorm(x, g):
    xf = x.astype(jnp.float32)
    y = xf * lax.rsqrt(jnp.mean(xf * xf, axis=-1, keepdims=True) + EPS)
    return (y * g.astype(jnp.float32)).astype(x.dtype)


def partial_rope(x, pos):
    half = ROT_DIM // 2
    inv = ROPE_THETA ** (-jnp.arange(half, dtype=jnp.float32) / half)
    ang = pos.astype(jnp.float32)[:, None] * inv[None, :]
    shape = (pos.shape[0],) + (1,) * (x.ndim - 3) + (half,)
    cos = jnp.cos(ang).reshape(shape).astype(x.dtype)
    sin = jnp.sin(ang).reshape(shape).astype(x.dtype)
    x1 = x[..., :half]
    x2 = x[..., half:ROT_DIM]
    return jnp.concatenate([x1 * cos - x2 * sin, x2 * cos + x1 * sin, x[..., ROT_DIM:]], axis=-1)


def rel_bias(table, dist):
    b = table[jnp.clip(dist, -REL_CLIP, REL_CLIP) + REL_CLIP]
    return jnp.moveaxis(b, -1, 0).astype(jnp.float32)


def mixer_inputs(xn, pos, w_in, qnorm_a, knorm_a, knorm_idx, qnorm_b, knorm_b):
    B, T, _ = xn.shape
    z = xn @ w_in
    qa, ka, va, qi, ki, wi, qb, kb, vb, ga, gb = jnp.split(z, SPLIT_POINTS, axis=-1)
    qa = partial_rope(rmsnorm(qa.reshape(B, T, H_A, HEAD_DIM), qnorm_a), pos)
    ka = partial_rope(rmsnorm(ka.reshape(B, T, H_A, HEAD_DIM), knorm_a), pos)
    va = va.reshape(B, T, H_A, HEAD_DIM)
    qi = partial_rope(qi.reshape(B, T, H_IDX, D_IDX), pos)
    ki = partial_rope(rmsnorm(ki, knorm_idx), pos)
    qb = rmsnorm(qb.reshape(B, T, H_B, HEAD_DIM), qnorm_b)
    kb = rmsnorm(kb.reshape(B, T, H_B, HEAD_DIM), knorm_b)
    vb = vb.reshape(B, T, H_B, HEAD_DIM)
    return qa, ka, va, qi, ki, wi, qb, kb, vb, jax.nn.sigmoid(ga), jax.nn.sigmoid(gb)


def dsa_attention(qa, ka_all, va_all, qi, ki_all, wi, q_pos, k_pos, topk):
    B, T = qa.shape[:2]
    qb = Q_BLOCK if T % Q_BLOCK == 0 else T
    nb = T // qb
    k_chunk = k_pos // CHUNK

    def block(args):
        q, q_i, w, qp = args
        s = jax.nn.relu(jnp.einsum('bqhd,bsd->bqhs', q_i, ki_all))
        isc = jnp.einsum('bqh,bqhs->bqs', w, s).astype(jnp.float32)
        q_chunk = qp // CHUNK
        vis = k_chunk[None, :] <= q_chunk[:, None]
        isc = jnp.where(vis[None], isc, -jnp.inf)
        _, idx = lax.top_k(isc, topk)
        k_sel = jax.vmap(lambda kk, ii: kk[ii])(ka_all, idx)
        v_sel = jax.vmap(lambda vv, ii: vv[ii])(va_all, idx)
        ok = k_chunk[idx] <= q_chunk[None, :, None]
        logits = jnp.einsum('bqhd,bqkhd->bhqk', q, k_sel).astype(jnp.float32) * ATTN_SCALE
        logits = jnp.where(ok[:, None], logits, -jnp.inf)
        p = jax.nn.softmax(logits, axis=-1).astype(v_sel.dtype)
        return jnp.einsum('bhqk,bqkhd->bqhd', p, v_sel)

    def blocks(a):
        return jnp.moveaxis(a.reshape((B, nb, qb) + a.shape[2:]), 1, 0)

    out = lax.map(block, (blocks(qa), blocks(qi), blocks(wi), q_pos.reshape(nb, qb)))
    return jnp.moveaxis(out, 0, 1).reshape(B, T, H_A, HEAD_DIM)


def band_attention_prompt(q, k, v, table):
    B, T, H, D = q.shape
    nc = T // CHUNK
    nbnd = BAND_CHUNKS + 1
    qc = q.reshape(B, nc, CHUNK, H, D)
    pad = jnp.zeros((B, BAND_CHUNKS, CHUNK, H, D), k.dtype)
    kp = jnp.concatenate([pad, k.reshape(B, nc, CHUNK, H, D)], axis=1)
    vp = jnp.concatenate([pad, v.reshape(B, nc, CHUNK, H, D)], axis=1)
    band_idx = jnp.arange(nc)[:, None] + jnp.arange(nbnd)[None, :]
    kband = kp[:, band_idx].reshape(B, nc, nbnd * CHUNK, H, D)
    vband = vp[:, band_idx].reshape(B, nc, nbnd * CHUNK, H, D)
    i = jnp.arange(CHUNK)
    s_rel = ((jnp.arange(nbnd)[:, None] - BAND_CHUNKS) * CHUNK + i[None, :]).reshape(-1)
    bias = rel_bias(table, i[:, None] - s_rel[None, :])
    key_ok = jnp.repeat(band_idx >= BAND_CHUNKS, CHUNK, axis=1)
    logits = jnp.einsum('bcqhd,bckhd->bchqk', qc, kband).astype(jnp.float32) * ATTN_SCALE + bias[None, None]
    logits = jnp.where(key_ok[None, :, None, None, :], logits, -jnp.inf)
    p = jax.nn.softmax(logits, axis=-1).astype(vband.dtype)
    return jnp.einsum('bchqk,bckhd->bcqhd', p, vband).reshape(B, T, H, D)


def band_attention_sample(q, k_all, v_all, q_pos, k_pos, table):
    bias = rel_bias(table, q_pos[:, None] - k_pos[None, :])
    logits = jnp.einsum('bqhd,bkhd->bhqk', q, k_all).astype(jnp.float32) * ATTN_SCALE + bias[None]
    p = jax.nn.softmax(logits, axis=-1).astype(v_all.dtype)
    return jnp.einsum('bhqk,bkhd->bqhd', p, v_all)


def merge_and_ffn(x, oa, ob, ga, gb, w_o_a, w_o_b, w_out, norm_ffn, w_up, w_down):
    B, T, _ = x.shape
    m = ga * (oa.reshape(B, T, W_A) @ w_o_a) + gb * (ob.reshape(B, T, W_B) @ w_o_b)
    x = x + m @ w_out
    h = jnp.square(jax.nn.relu(rmsnorm(x, norm_ffn) @ w_up))
    return x + h @ w_down


def setup_inputs(seed: int = 0) -> dict:
    key = jax.random.key(seed)
    ks = jax.random.split(key, 24)
    f32 = jnp.float32

    def nrm(k, shape, scale=1.0):
        return jax.random.normal(k, shape, f32) * scale

    def gain(k, n):
        return 1.0 + 0.01 * jax.random.normal(k, (DEPTH, n), f32)

    rows_b = min(BAND_ROWS, PAST_LEN)
    return {
        'x_prompt': nrm(ks[0], (BATCH, SEQ, D_MODEL)),
        'x_sample': nrm(ks[1], (DEC_BATCH, DEC_SEQ, D_MODEL)),
        'cache_k_a': nrm(ks[2], (DEPTH, DEC_BATCH, PAST_LEN, H_A, HEAD_DIM)),
        'cache_v_a': nrm(ks[3], (DEPTH, DEC_BATCH, PAST_LEN, H_A, HEAD_DIM)),
        'cache_k_idx': nrm(ks[4], (DEPTH, DEC_BATCH, PAST_LEN, D_IDX)),
        'cache_k_b': nrm(ks[5], (DEPTH, DEC_BATCH, rows_b, H_B, HEAD_DIM)),
        'cache_v_b': nrm(ks[6], (DEPTH, DEC_BATCH, rows_b, H_B, HEAD_DIM)),
        'norm_mix': gain(ks[7], D_MODEL),
        'w_in': nrm(ks[8], (DEPTH, D_MODEL, D_IN), D_MODEL ** -0.5),
        'qnorm_a': gain(ks[9], HEAD_DIM),
        'knorm_a': gain(ks[10], HEAD_DIM),
        'knorm_idx': gain(ks[11], D_IDX),
        'qnorm_b': gain(ks[12], HEAD_DIM),
        'knorm_b': gain(ks[13], HEAD_DIM),
        'rel_bias_b': nrm(ks[14], (DEPTH, 2 * REL_CLIP + 1, H_B), 0.5),
        'w_o_a': nrm(ks[15], (DEPTH, W_A, D_MODEL), W_A ** -0.5),
        'w_o_b': nrm(ks[16], (DEPTH, W_B, D_MODEL), W_B ** -0.5),
        'w_out': nrm(ks[17], (DEPTH, D_MODEL, D_MODEL), D_MODEL ** -0.5),
        'norm_ffn': gain(ks[18], D_MODEL),
        'w_up': nrm(ks[19], (DEPTH, D_MODEL, D_FF), D_MODEL ** -0.5),
        'w_down': nrm(ks[20], (DEPTH, D_FF, D_MODEL), D_FF ** -0.5),
    }


def reference(x_prompt, x_sample, cache_k_a, cache_v_a, cache_k_idx, cache_k_b, cache_v_b,
              norm_mix, w_in, qnorm_a, knorm_a, knorm_idx, qnorm_b, knorm_b, rel_bias_b,
              w_o_a, w_o_b, w_out, norm_ffn, w_up, w_down):
    T = x_prompt.shape[1]
    Ts = x_sample.shape[1]
    past = cache_k_a.shape[2]
    rows_b = cache_k_b.shape[2]
    pos_p = jnp.arange(T, dtype=jnp.int32)
    pos_s = past + jnp.arange(Ts, dtype=jnp.int32)
    kpos_a_s = jnp.arange(past + Ts, dtype=jnp.int32)
    kpos_b_s = jnp.arange(past - rows_b, past + Ts, dtype=jnp.int32)
    topk_p = min(TOPK_MAX, T // 4)
    topk_s = min(TOPK_MAX, (past + Ts) // 4)
    keep_p = min(BAND_ROWS, T)

    xp, xs = x_prompt, x_sample
    ka_p, va_p, ki_p, kb_p, vb_p = [], [], [], [], []
    ka_s, va_s, ki_s, kb_s, vb_s = [], [], [], [], []
    for l in range(DEPTH):
        norms = (qnorm_a[l], knorm_a[l], knorm_idx[l], qnorm_b[l], knorm_b[l])
        outw = (w_o_a[l], w_o_b[l], w_out[l], norm_ffn[l], w_up[l], w_down[l])
        qa, ka, va, qi, ki, wi, qb, kb, vb, ga, gb = mixer_inputs(rmsnorm(xp, norm_mix[l]), pos_p, w_in[l], *norms)
        oa = dsa_attention(qa, ka, va, qi, ki, wi, pos_p, pos_p, topk_p)
        ob = band_attention_prompt(qb, kb, vb, rel_bias_b[l])
        xp = merge_and_ffn(xp, oa, ob, ga, gb, *outw)
        ka_p.append(ka); va_p.append(va); ki_p.append(ki)
        kb_p.append(kb[:, T - keep_p:]); vb_p.append(vb[:, T - keep_p:])
        qa, ka, va, qi, ki, wi, qb, kb, vb, ga, gb = mixer_inputs(rmsnorm(xs, norm_mix[l]), pos_s, w_in[l], *norms)
        oa = dsa_attention(qa, jnp.concatenate([cache_k_a[l], ka], axis=1),
                           jnp.concatenate([cache_v_a[l], va], axis=1), qi,
                           jnp.concatenate([cache_k_idx[l], ki], axis=1), wi, pos_s, kpos_a_s, topk_s)
        ob = band_attention_sample(qb, jnp.concatenate([cache_k_b[l], kb], axis=1),
                                   jnp.concatenate([cache_v_b[l], vb], axis=1), pos_s, kpos_b_s, rel_bias_b[l])
        xs = merge_and_ffn(xs, oa, ob, ga, gb, *outw)
        ka_s.append(ka); va_s.append(va); ki_s.append(ki); kb_s.append(kb); vb_s.append(vb)

    return (xp, xs,
            jnp.stack(ka_p), jnp.stack(va_p), jnp.stack(ki_p), jnp.stack(kb_p), jnp.stack(vb_p),
            jnp.stack(ka_s), jnp.stack(va_s), jnp.stack(ki_s), jnp.stack(kb_s), jnp.stack(vb_s))
```

```python
import functools

import jax
import jax.numpy as jnp
from jax import lax
from jax.experimental import pallas as pl
from jax.experimental.pallas import tpu as pltpu

CHUNK = 64
HEAD_DIM = 64
N_HEADS = 8
D_IDX = 64
ROT_DIM = HEAD_DIM // 4
ROPE_THETA = 500000.0
TOPK_MAX = 256
BAND_CHUNKS = 8
BAND_ROWS = BAND_CHUNKS * CHUNK
REL_CLIP = 128
EPS = 1e-6
ATTN_SCALE = HEAD_DIM ** -0.5
W_HEADS = N_HEADS * HEAD_DIM

LANES = 128
NEG = -1e30
INT_MIN = -2 ** 31
INT_MAX = 2 ** 31 - 1
VMEM_LIMIT = 56 * 1024 * 1024

F32 = jnp.float32
BF16 = jnp.bfloat16
I32 = jnp.int32

_NT = (((1,), (1,)), ((), ()))


def _const_spec(shape):
    nd = len(shape)
    return pl.BlockSpec(shape, lambda *_: (0,) * nd, pipeline_mode=pl.Buffered(1))


def _proj_kernel(x_ref, gmix_ref, w1_ref, w2_ref, w3_ref, w4_ref, seg_ref,
                 gqa_ref, gka_ref, gki_ref, gqb_ref, gkb_ref,
                 cos_ref, s1_ref, s2_ref,
                 qa_o, ka_o, kab_o, va_o, vab_o, qi_o, kiwi_o, kib_o,
                 qb_o, kb_o, kbb_o, vb_o, vbb_o, gate_o):
    x = x_ref[...]
    ms = jnp.mean(x * x, axis=-1, keepdims=True)
    xn = (x * lax.rsqrt(ms + EPS) * gmix_ref[...]).astype(BF16)

    def mm(w_ref, lo, hi):
        return jnp.dot(xn, w_ref[:, lo:hi], preferred_element_type=F32)

    def head_norm(z, gain_ref):
        msq = jnp.dot((z * z).astype(BF16), seg_ref[...], preferred_element_type=F32)
        return z * lax.rsqrt(msq + EPS) * gain_ref[...]

    cos1, sa1, sb1 = cos_ref[...], s1_ref[...], s2_ref[...]
    cos4 = jnp.concatenate([cos1] * 4, axis=1)
    sa4 = jnp.concatenate([sa1] * 4, axis=1)
    sb4 = jnp.concatenate([sb1] * 4, axis=1)

    def rope(z, c, sa, sb):
        w = z.shape[1]
        return z * c + pltpu.roll(z, ROT_DIM // 2, 1) * sa + pltpu.roll(z, w - ROT_DIM // 2, 1) * sb

    W = W_HEADS
    qa = rope(head_norm(mm(w1_ref, 0, W), gqa_ref), cos4, sa4, sb4)
    qa_o[...] = (qa * ATTN_SCALE).astype(BF16)
    ka = rope(head_norm(mm(w1_ref, W, 2 * W), gka_ref), cos4, sa4, sb4)
    ka_o[...] = ka
    kab_o[...] = ka.astype(BF16)
    va = mm(w1_ref, 2 * W, 3 * W)
    va_o[...] = va
    vab_o[...] = va.astype(BF16)
    qi_o[...] = rope(mm(w1_ref, 3 * W, 4 * W), cos4, sa4, sb4).astype(BF16)

    z2 = mm(w2_ref, 0, LANES)
    lane = lax.broadcasted_iota(I32, z2.shape, 1)
    is_ki = lane < D_IDX
    ms_ki = jnp.sum(jnp.where(is_ki, z2 * z2, 0.0), axis=-1, keepdims=True) * (1.0 / D_IDX)
    ki = rope(z2 * lax.rsqrt(ms_ki + EPS) * gki_ref[...], cos1, sa1, sb1)
    kiwi = jnp.where(is_ki, ki, z2)
    kiwi_o[...] = kiwi
    kib_o[...] = kiwi[:, :D_IDX].astype(BF16)

    qb = head_norm(mm(w3_ref, 0, W), gqb_ref)
    qb_o[...] = (qb * ATTN_SCALE).astype(BF16)
    kb = head_norm(mm(w3_ref, W, 2 * W), gkb_ref)
    kb_o[...] = kb
    kbb_o[...] = kb.astype(BF16)
    vb = mm(w3_ref, 2 * W, 3 * W)
    vb_o[...] = vb
    vbb_o[...] = vb.astype(BF16)

    d2 = w4_ref.shape[1]
    for c in range(d2 // W):
        zg = mm(w4_ref, c * W, (c + 1) * W)
        gate_o[:, c * W:(c + 1) * W] = 1.0 / (1.0 + jnp.exp(-zg))


def _project(x2d, tables, pw, tm):
    n, d = x2d.shape
    ttab = tables[0].shape[0]
    nt = ttab // tm
    W = W_HEADS
    row = lambda i: (i, 0)
    tab = lambda i: (i % nt, 0)
    d2 = pw['w4'].shape[1]
    in_specs = [
        pl.BlockSpec((tm, d), row),
        _const_spec((1, d)),
        _const_spec(pw['w1'].shape), _const_spec(pw['w2'].shape),
        _const_spec(pw['w3'].shape), _const_spec(pw['w4'].shape),
        _const_spec((W, W)),
        _const_spec((1, W)), _const_spec((1, W)), _const_spec((1, LANES)),
        _const_spec((1, W)), _const_spec((1, W)),
        pl.BlockSpec((tm, LANES), tab), pl.BlockSpec((tm, LANES), tab), pl.BlockSpec((tm, LANES), tab),
    ]
    outs = [
        ('qa', W, BF16), ('ka', W, F32), ('ka_bf', W, BF16), ('va', W, F32), ('va_bf', W, BF16),
        ('qi', W, BF16), ('kiwi', LANES, F32), ('ki_bf', D_IDX, BF16),
        ('qb', W, BF16), ('kb', W, F32), ('kb_bf', W, BF16), ('vb', W, F32), ('vb_bf', W, BF16),
        ('gate', d2, F32),
    ]
    res = pl.pallas_call(
        _proj_kernel,
        grid=(n // tm,),
        in_specs=in_specs,
        out_specs=[pl.BlockSpec((tm, w), row) for _, w, _ in outs],
        out_shape=[jax.ShapeDtypeStruct((n, w), dt) for _, w, dt in outs],
        compiler_params=pltpu.CompilerParams(
            dimension_semantics=("arbitrary",), vmem_limit_bytes=VMEM_LIMIT),
        name="proj",
    )(x2d, pw['gmix'], pw['w1'], pw['w2'], pw['w3'], pw['w4'], pw['seg'],
      pw['gqa'], pw['gka'], pw['gki'], pw['gqb'], pw['gkb'], *tables)
    return {name: r for (name, _, _), r in zip(outs, res)}


def _dsa_kernel(qa_ref, qi_ref, kiwi_ref, kidx_ref, ka_ref, va_ref, o_ref,
                ikey_ref, qis_ref, qpad_ref, wb_ref, lo_ref, hi_ref, clo_ref,
                m_ref, l_ref, acc_ref, *, tq, kb_size, s_valid, q_pos0, topk, rg):
    H = N_HEADS
    cpb = kb_size // LANES
    j = pl.program_id(1)
    q0 = q_pos0 + j * tq
    last_chunk = jnp.right_shift(q0 + (tq - 1), 6)
    n_vis_max = jnp.minimum((last_chunk + 1) * CHUNK, s_valid)
    n_kb = jnp.right_shift(n_vis_max + (kb_size - 1), kb_size.bit_length() - 1)

    lane = lax.broadcasted_iota(I32, (tq, LANES), 1)
    row_pos = q0 + lax.broadcasted_iota(I32, (tq, LANES), 0)
    row_chunk = jnp.right_shift(row_pos, 6)

    qi = qi_ref[0]
    qa = qa_ref[0]
    kiwi = kiwi_ref[0]
    for h in range(H):
        qis_ref[h * tq:(h + 1) * tq, :] = qi[:, h * D_IDX:(h + 1) * D_IDX]
        pair = qa[:, (h // 2) * LANES:(h // 2 + 1) * LANES]
        own = (lane >= HEAD_DIM) if (h % 2) else (lane < HEAD_DIM)
        qpad_ref[h] = jnp.where(own, pair, jnp.zeros_like(pair))
        wb_ref[h] = jnp.broadcast_to(kiwi[:, D_IDX + h:D_IDX + h + 1], (tq, LANES))

    def score_block(kb, carry):
        k0 = pl.multiple_of(kb * kb_size, kb_size)
        kblk = kidx_ref[0, pl.ds(k0, kb_size), :]
        s = lax.dot_general(qis_ref[...], kblk, _NT, preferred_element_type=F32)
        for c in range(cpb):
            acc = None
            for h in range(H):
                r = jnp.maximum(s[h * tq:(h + 1) * tq, c * LANES:(c + 1) * LANES], 0.0) * wb_ref[h]
                acc = r if acc is None else acc + r
            kpos = k0 + c * LANES + lane
            vis = (jnp.right_shift(kpos, 6) <= row_chunk) & (kpos < s_valid)
            b = lax.bitcast_convert_type(acc, I32)
            key = b ^ (jnp.right_shift(b, 31) & INT_MAX)
            ikey_ref[kb * cpb + c] = jnp.where(vis, key, INT_MIN)
        return carry

    lax.fori_loop(0, n_kb, score_block, 0)

    lo_ref[...] = jnp.full((tq, LANES), INT_MIN + 1, I32)
    hi_ref[...] = jnp.full((tq, LANES), INT_MAX, I32)
    clo_ref[...] = jnp.minimum((row_chunk + 1) * CHUNK, s_valid)

    def count_ge(r0, t):
        def body(kb, cnt):
            for c in range(cpb):
                blk = ikey_ref[kb * cpb + c, r0:r0 + rg, :]
                cnt = cnt + (blk >= t).astype(I32)
            return cnt
        cnt = lax.fori_loop(0, n_kb, body, jnp.zeros((rg, LANES), I32))
        return jnp.sum(cnt, axis=1, keepdims=True)

    def bisect_cond(carry):
        it, done = carry
        return (it < 34) & (done == 0)

    def bisect_body(carry):
        it, _ = carry
        done = jnp.int32(1)
        for g in range(tq // rg):
            r0 = g * rg
            lo = lo_ref[r0:r0 + rg, :]
            hi = hi_ref[r0:r0 + rg, :]
            clo = clo_ref[r0:r0 + rg, :]
            mid = jnp.right_shift(lo, 1) + jnp.right_shift(hi, 1) + (lo & hi & 1)
            c = jnp.broadcast_to(count_ge(r0, mid), (rg, LANES))
            ge = c >= topk
            lo_n = jnp.where(ge, mid, lo)
            hi_n = jnp.where(ge, hi, mid)
            clo_n = jnp.where(ge, c, clo)
            lo_ref[r0:r0 + rg, :] = lo_n
            hi_ref[r0:r0 + rg, :] = hi_n
            clo_ref[r0:r0 + rg, :] = clo_n
            conv = (clo_n <= topk) | (lo_n + 1 >= hi_n)
            done = done & jnp.min(conv.astype(I32))
        return it + 1, done

    lax.while_loop(bisect_cond, bisect_body, (jnp.int32(0), jnp.int32(0)))

    has_ties = jnp.max(clo_ref[...]) > topk

    @pl.when(has_ties)
    def _():
        tri = (lax.broadcasted_iota(I32, (LANES, LANES), 0)
               <= lax.broadcasted_iota(I32, (LANES, LANES), 1)).astype(BF16)
        for g in range(tq // rg):
            r0 = g * rg
            tau = lo_ref[r0:r0 + rg, :]
            c_gt = jnp.broadcast_to(count_ge(r0, tau + 1), (rg, LANES))
            need = (topk - c_gt).astype(F32)

            def drop_block(col, run):
                blk = ikey_ref[col, r0:r0 + rg, :]
                eq = blk == tau
                pref = jnp.dot(eq.astype(BF16), tri, preferred_element_type=F32)
                keep = (run + pref) <= need
                ikey_ref[col, r0:r0 + rg, :] = jnp.where(eq & jnp.logical_not(keep), INT_MIN, blk)
                return run + jnp.broadcast_to(pref[:, LANES - 1:LANES], (rg, LANES))

            lax.fori_loop(0, n_kb * cpb, drop_block, jnp.zeros((rg, LANES), F32))

    m_ref[...] = jnp.full(m_ref.shape, NEG, F32)
    l_ref[...] = jnp.zeros(l_ref.shape, F32)
    acc_ref[...] = jnp.zeros(acc_ref.shape, F32)

    def attend_block(kb, carry):
        k0 = pl.multiple_of(kb * kb_size, kb_size)
        tau = lo_ref[...]
        mb = jnp.concatenate(
            [jnp.where(ikey_ref[kb * cpb + c] >= tau, 0.0, NEG) for c in range(cpb)], axis=1)
        for h in range(H):
            p0 = (h // 2) * LANES
            kblk = ka_ref[0, pl.ds(k0, kb_size), p0:p0 + LANES]
            vblk = va_ref[0, pl.ds(k0, kb_size), p0:p0 + LANES]
            s = lax.dot_general(qpad_ref[h], kblk, _NT, preferred_element_type=F32) + mb
            m_prev = m_ref[h]
            m_new = jnp.maximum(m_prev, jnp.max(s, axis=1, keepdims=True))
            alpha = jnp.exp(m_prev - m_new)
            p = jnp.exp(s - m_new)
            l_ref[h] = alpha * l_ref[h] + jnp.sum(p, axis=1, keepdims=True)
            acc_ref[h] = alpha * acc_ref[h] + jnp.dot(p.astype(BF16), vblk, preferred_element_type=F32)
            m_ref[h] = m_new
        return carry

    lax.fori_loop(0, n_kb, attend_block, 0)

    for pr in range(H // 2):
        even = acc_ref[2 * pr] / l_ref[2 * pr]
        odd = acc_ref[2 * pr + 1] / l_ref[2 * pr + 1]
        o_ref[0, :, pr * LANES:(pr + 1) * LANES] = jnp.where(lane < HEAD_DIM, even, odd).astype(BF16)


def _dsa(qa, qi, kiwi, kidx, ka, va, *, tq, kb_size, s_valid, q_pos0, topk):
    b, t, w = qa.shape
    s_pad = kidx.shape[1]
    assert t % tq == 0 and s_pad % kb_size == 0 and kb_size % LANES == 0
    assert kb_size & (kb_size - 1) == 0
    rg = min(tq, 64)
    qspec = lambda width: pl.BlockSpec((1, tq, width), lambda bi, ji: (bi, ji, 0))
    kspec = lambda width: pl.BlockSpec((1, s_pad, width), lambda bi, ji: (bi, 0, 0),
                                       pipeline_mode=pl.Buffered(1))
    kern = functools.partial(_dsa_kernel, tq=tq, kb_size=kb_size, s_valid=s_valid,
                             q_pos0=q_pos0, topk=topk, rg=rg)
    return pl.pallas_call(
        kern,
        grid=(b, t // tq),
        in_specs=[qspec(w), qspec(w), qspec(LANES), kspec(D_IDX), kspec(w), kspec(w)],
        out_specs=qspec(w),
        out_shape=jax.ShapeDtypeStruct((b, t, w), BF16),
        scratch_shapes=[
            pltpu.VMEM((s_pad // LANES, tq, LANES), I32),
            pltpu.VMEM((N_HEADS * tq, D_IDX), BF16),
            pltpu.VMEM((N_HEADS, tq, LANES), BF16),
            pltpu.VMEM((N_HEADS, tq, LANES), F32),
            pltpu.VMEM((tq, LANES), I32),
            pltpu.VMEM((tq, LANES), I32),
            pltpu.VMEM((tq, LANES), I32),
            pltpu.VMEM((N_HEADS, tq, 1), F32),
            pltpu.VMEM((N_HEADS, tq, 1), F32),
            pltpu.VMEM((N_HEADS, tq, LANES), F32),
        ],
        compiler_params=pltpu.CompilerParams(
            dimension_semantics=("arbitrary", "arbitrary"), vmem_limit_bytes=VMEM_LIMIT),
        name="dsa",
    )(qa, qi, kiwi, kidx, ka, va)


def _band_kernel(*refs, tq, kb_size, nkb):
    q_ref = refs[0]
    k_refs = refs[1:1 + nkb]
    v_refs = refs[1 + nkb:1 + 2 * nkb]
    bias_ref = refs[1 + 2 * nkb]
    o_ref = refs[2 + 2 * nkb]
    j = pl.program_id(1)
    q = q_ref[0]
    lane = lax.broadcasted_iota(I32, (tq, LANES), 1)
    for pr in range(N_HEADS // 2):
        p0 = pr * LANES
        qpair = q[:, p0:p0 + LANES]
        outs = []
        for par in range(2):
            h = 2 * pr + par
            own = (lane >= HEAD_DIM) if par else (lane < HEAD_DIM)
            qp = jnp.where(own, qpair, jnp.zeros_like(qpair))
            parts = []
            for i in range(nkb):
                s = lax.dot_general(qp, k_refs[i][0, :, p0:p0 + LANES], _NT, preferred_element_type=F32)
                s = s + bias_ref[h, :, i * kb_size:(i + 1) * kb_size]
                if i < nkb - 1:
                    s = jnp.where(j + (i - (nkb - 1)) >= 0, s, NEG)
                parts.append(s)
            m = parts[0].max(axis=1, keepdims=True)
            for s in parts[1:]:
                m = jnp.maximum(m, s.max(axis=1, keepdims=True))
            l = jnp.zeros((tq, 1), F32)
            o = jnp.zeros((tq, LANES), F32)
            for i in range(nkb):
                p = jnp.exp(parts[i] - m)
                l = l + jnp.sum(p, axis=1, keepdims=True)
                o = o + jnp.dot(p.astype(BF16), v_refs[i][0, :, p0:p0 + LANES], preferred_element_type=F32)
            outs.append(o / l)
        o_ref[0, :, p0:p0 + LANES] = jnp.where(lane < HEAD_DIM, outs[0], outs[1]).astype(BF16)


def _band(q, k, v, bias, *, tq, kb_size, nkb):
    b, t, w = q.shape
    qspec = pl.BlockSpec((1, tq, w), lambda bi, ji: (bi, ji, 0))

    def kspec(i):
        off = i - (nkb - 1)
        return pl.BlockSpec((1, kb_size, w), lambda bi, ji: (bi, jnp.maximum(ji + off, 0), 0))

    kern = functools.partial(_band_kernel, tq=tq, kb_size=kb_size, nkb=nkb)
    return pl.pallas_call(
        kern,
        grid=(b, t // tq),
        in_specs=[qspec] + [kspec(i) for i in range(nkb)] * 2 + [_const_spec(bias.shape)],
        out_specs=qspec,
        out_shape=jax.ShapeDtypeStruct((b, t, w), BF16),
        compiler_params=pltpu.CompilerParams(
            dimension_semantics=("arbitrary", "arbitrary"), vmem_limit_bytes=VMEM_LIMIT),
        name="band",
    )(q, *([k] * nkb), *([v] * nkb), bias)


def _band_bias(table, tq, nk):
    r = jnp.arange(tq)[:, None]
    c = jnp.arange(nk)[None, :]
    dist = r - c + BAND_ROWS
    b = table[jnp.clip(dist, -REL_CLIP, REL_CLIP) + REL_CLIP]
    qc, kc = r // CHUNK, c // CHUNK
    ok = (kc >= qc) & (kc <= qc + BAND_CHUNKS)
    return jnp.where(ok[None], jnp.moveaxis(b, -1, 0).astype(F32), NEG)


def _ffn_kernel(x_ref, oa_ref, ob_ref, gate_ref, woa_ref, wob_ref, wout_ref, gffn_ref,
                wup_ref, wdn_ref, y_ref, *, ff_chunk):
    d = x_ref.shape[1]
    a = jnp.dot(oa_ref[...], woa_ref[...], preferred_element_type=F32)
    b = jnp.dot(ob_ref[...], wob_ref[...], preferred_element_type=F32)
    m = gate_ref[:, :d] * a + gate_ref[:, d:] * b
    x1 = x_ref[...] + jnp.dot(m.astype(BF16), wout_ref[...], preferred_element_type=F32)
    ms = jnp.mean(x1 * x1, axis=-1, keepdims=True)
    xn = (x1 * lax.rsqrt(ms + EPS) * gffn_ref[...]).astype(BF16)
    acc = x1
    for c in range(wup_ref.shape[1] // ff_chunk):
        h = jnp.dot(xn, wup_ref[:, c * ff_chunk:(c + 1) * ff_chunk], preferred_element_type=F32)
        h = jnp.square(jnp.maximum(h, 0.0))
        acc = acc + jnp.dot(h.astype(BF16), wdn_ref[c * ff_chunk:(c + 1) * ff_chunk, :],
                            preferred_element_type=F32)
    y_ref[...] = acc


def _ffn(x2d, oa, ob, gate, fw, tm):
    n, d = x2d.shape
    w = oa.shape[1]
    row = lambda i: (i, 0)
    kern = functools.partial(_ffn_kernel, ff_chunk=min(1024, fw['wup'].shape[1]))
    return pl.pallas_call(
        kern,
        grid=(n // tm,),
        in_specs=[pl.BlockSpec((tm, d), row), pl.BlockSpec((tm, w), row), pl.BlockSpec((tm, w), row),
                  pl.BlockSpec((tm, 2 * d), row),
                  _const_spec(fw['woa'].shape), _const_spec(fw['wob'].shape), _const_spec(fw['wout'].shape),
                  _const_spec((1, d)), _const_spec(fw['wup'].shape), _const_spec(fw['wdn'].shape)],
        out_specs=pl.BlockSpec((tm, d), row),
        out_shape=jax.ShapeDtypeStruct((n, d), F32),
        compiler_params=pltpu.CompilerParams(
            dimension_semantics=("arbitrary",), vmem_limit_bytes=VMEM_LIMIT),
        name="ffn",
    )(x2d, oa, ob, gate, fw['woa'], fw['wob'], fw['wout'], fw['gffn'], fw['wup'], fw['wdn'])


def _rope_tables(pos):
    half = ROT_DIM // 2
    inv = ROPE_THETA ** (-jnp.arange(half, dtype=F32) / half)
    ang = pos.astype(F32)[:, None] * inv[None, :]
    r = jnp.arange(LANES) % HEAD_DIM
    cos = jnp.cos(ang)[:, r % half]
    sin = jnp.sin(ang)[:, r % half]
    in_rot = (r < ROT_DIM)[None, :]
    first = (r < half)[None, :]
    c = jnp.where(in_rot, cos, 1.0)
    sa = jnp.where(in_rot & ~first, sin, 0.0)
    sb = jnp.where(first, -sin, 0.0)
    return c, sa, sb


def _tile_gain(g):
    return jnp.tile(g.astype(F32), N_HEADS)[None, :]


def _pick_tile(n, pref):
    t = min(n, pref)
    while n % t:
        t //= 2
    return t


def kernel(x_prompt, x_sample, cache_k_a, cache_v_a, cache_k_idx, cache_k_b, cache_v_b, norm_mix, w_in,
           qnorm_a, knorm_a, knorm_idx, qnorm_b, knorm_b, rel_bias_b, w_o_a, w_o_b, w_out, norm_ffn,
           w_up, w_down):
    B, T, D = x_prompt.shape
    Bs, Ts, _ = x_sample.shape
    depth = w_in.shape[0]
    past = cache_k_a.shape[2]
    rows_b = cache_k_b.shape[2]
    W = W_HEADS
    H = N_HEADS
    assert T % CHUNK == 0 and Ts % CHUNK == 0 and past % CHUNK == 0
    assert rows_b == BAND_ROWS and T >= BAND_ROWS
    topk_p = min(TOPK_MAX, T // 4)
    topk_s = min(TOPK_MAX, (past + Ts) // 4)
    keep_p = min(BAND_ROWS, T)

    pos_p = jnp.arange(T, dtype=jnp.int32)
    pos_s = past + jnp.arange(Ts, dtype=jnp.int32)
    tab_p = _rope_tables(pos_p)
    tab_s = tuple(jnp.tile(t, (Bs, 1)) for t in _rope_tables(pos_s))

    seg = jnp.where((jnp.arange(W)[:, None] // HEAD_DIM) == (jnp.arange(W)[None, :] // HEAD_DIM),
                    1.0 / HEAD_DIM, 0.0).astype(BF16)

    xp = x_prompt.reshape(B * T, D)
    xs = x_sample.reshape(Bs * Ts, D)
    outs_p = [[] for _ in range(5)]
    outs_s = [[] for _ in range(5)]

    tq_p = _pick_tile(T, 256)
    band_tq = _pick_tile(T, 256)
    assert BAND_ROWS % band_tq == 0

    for l in range(depth):
        wl = w_in[l].astype(BF16)
        o_ki = 4 * W
        o_b = o_ki + D_IDX + H
        o_g = o_b + 3 * W
        w2 = jnp.pad(wl[:, o_ki:o_b], ((0, 0), (0, LANES - D_IDX - H)))
        pw = dict(
            gmix=norm_mix[l][None, :].astype(F32),
            w1=wl[:, :o_ki], w2=w2, w3=wl[:, o_b:o_g], w4=wl[:, o_g:], seg=seg,
            gqa=_tile_gain(qnorm_a[l]), gka=_tile_gain(knorm_a[l]),
            gki=jnp.pad(knorm_idx[l].astype(F32), (0, LANES - D_IDX))[None, :],
            gqb=_tile_gain(qnorm_b[l]), gkb=_tile_gain(knorm_b[l]),
        )
        fw = dict(woa=w_o_a[l].astype(BF16), wob=w_o_b[l].astype(BF16), wout=w_out[l].astype(BF16),
                  gffn=norm_ffn[l][None, :].astype(F32), wup=w_up[l].astype(BF16),
                  wdn=w_down[l].astype(BF16))
        table = rel_bias_b[l]

        pp = _project(xp, tab_p, pw, _pick_tile(T, 256))
        r3 = lambda a: a.reshape(B, T, a.shape[-1])
        oa = _dsa(r3(pp['qa']), r3(pp['qi']), r3(pp['kiwi']), r3(pp['ki_bf']), r3(pp['ka_bf']),
                  r3(pp['va_bf']), tq=tq_p, kb_size=min(512, tq_p), s_valid=T, q_pos0=0, topk=topk_p)
        nkb = BAND_ROWS // band_tq + 1
        ob = _band(r3(pp['qb']), r3(pp['kb_bf']), r3(pp['vb_bf']),
                   _band_bias(table, band_tq, nkb * band_tq), tq=band_tq, kb_size=band_tq, nkb=nkb)
        xp = _ffn(xp, oa.reshape(B * T, W), ob.reshape(B * T, W), pp['gate'], fw, _pick_tile(B * T, 256))
        outs_p[0].append(pp['ka'].reshape(B, T, H, HEAD_DIM))
        outs_p[1].append(pp['va'].reshape(B, T, H, HEAD_DIM))
        outs_p[2].append(r3(pp['kiwi'])[:, :, :D_IDX])
        outs_p[3].append(pp['kb'].reshape(B, T, H, HEAD_DIM)[:, T - keep_p:])
        outs_p[4].append(pp['vb'].reshape(B, T, H, HEAD_DIM)[:, T - keep_p:])

        ps = _project(xs, tab_s, pw, _pick_tile(Bs * Ts, 256))
        s3 = lambda a: a.reshape(Bs, Ts, a.shape[-1])
        s_all = past + Ts
        kb_s = 128
        s_pad = -(-s_all // kb_s) * kb_s

        def with_cache(cache, new, width):
            full = jnp.concatenate([cache.reshape(Bs, -1, width).astype(BF16), s3(new)], axis=1)
            return jnp.pad(full, ((0, 0), (0, s_pad - s_all), (0, 0)))

        oa = _dsa(s3(ps['qa']), s3(ps['qi']), s3(ps['kiwi']),
                  with_cache(cache_k_idx[l], ps['ki_bf'], D_IDX),
                  with_cache(cache_k_a[l], ps['ka_bf'], W), with_cache(cache_v_a[l], ps['va_bf'], W),
                  tq=Ts, kb_size=kb_s, s_valid=s_all, q_pos0=past, topk=topk_s)
        kb_all = jnp.concatenate([cache_k_b[l].reshape(Bs, rows_b, W).astype(BF16), s3(ps['kb_bf'])], axis=1)
        vb_all = jnp.concatenate([cache_v_b[l].reshape(Bs, rows_b, W).astype(BF16), s3(ps['vb_bf'])], axis=1)
        ob = _band(s3(ps['qb']), kb_all, vb_all, _band_bias(table, Ts, rows_b + Ts),
                   tq=Ts, kb_size=rows_b + Ts, nkb=1)
        xs = _ffn(xs, oa.reshape(Bs * Ts, W), ob.reshape(Bs * Ts, W), ps['gate'], fw,
                  _pick_tile(Bs * Ts, 256))
        outs_s[0].append(ps['ka'].reshape(Bs, Ts, H, HEAD_DIM))
        outs_s[1].append(ps['va'].reshape(Bs, Ts, H, HEAD_DIM))
        outs_s[2].append(s3(ps['kiwi'])[:, :, :D_IDX])
        outs_s[3].append(ps['kb'].reshape(Bs, Ts, H, HEAD_DIM))
        outs_s[4].append(ps['vb'].reshape(Bs, Ts, H, HEAD_DIM))

    return (xp.reshape(B, T, D), xs.reshape(Bs, Ts, D),
            *[jnp.stack(o) for o in outs_p], *[jnp.stack(o) for o in outs_s])
```

```python
import functools

import jax
import jax.numpy as jnp
from jax import lax
from jax.experimental import pallas as pl
from jax.experimental.pallas import tpu as pltpu

CHUNK = 64
HEAD_DIM = 64
N_HEADS = 8
D_IDX = 64
ROT_DIM = HEAD_DIM // 4
ROPE_THETA = 500000.0
TOPK_MAX = 256
BAND_CHUNKS = 8
BAND_ROWS = BAND_CHUNKS * CHUNK
REL_CLIP = 128
EPS = 1e-6
ATTN_SCALE = HEAD_DIM ** -0.5
W_HEADS = N_HEADS * HEAD_DIM

LANES = 128
SUBLANES = 8
ONES_ROWS = 16
PAIR_ROWS = LANES + ONES_ROWS
VT_ROWS = (N_HEADS // 2) * PAIR_ROWS
L_FLOOR = 1e-30
NEG = -1e30
INT_MIN = -2 ** 31
INT_MAX = 2 ** 31 - 1
VMEM_LIMIT = 56 * 1024 * 1024

F32 = jnp.float32
BF16 = jnp.bfloat16
I32 = jnp.int32

_NT = (((1,), (1,)), ((), ()))


def _const_spec(shape):
    nd = len(shape)
    return pl.BlockSpec(shape, lambda *_: (0,) * nd, pipeline_mode=pl.Buffered(1))


def _proj_kernel(x_ref, gmix_ref, w1_ref, w2_ref, w3_ref, w4_ref, wvt_ref, seg_ref,
                 gqa_ref, gka_ref, gki_ref, gqb_ref, gkb_ref,
                 cos_ref, s1_ref, s2_ref,
                 qa_o, ka_o, kab_o, va_o, vat_o, qi_o, kiwi_o, kib_o,
                 qb_o, kb_o, kbb_o, vb_o, vbb_o, gate_o):
    x = x_ref[...]
    ms = jnp.mean(x * x, axis=-1, keepdims=True)
    xn = (x * lax.rsqrt(ms + EPS) * gmix_ref[...]).astype(BF16)

    def mm(w_ref, lo, hi):
        return jnp.dot(xn, w_ref[:, lo:hi], preferred_element_type=F32)

    def head_norm(z, gain_ref):
        msq = jnp.dot((z * z).astype(BF16), seg_ref[...], preferred_element_type=F32)
        return z * lax.rsqrt(msq + EPS) * gain_ref[...]

    cos1, sa1, sb1 = cos_ref[...], s1_ref[...], s2_ref[...]
    cos4 = jnp.concatenate([cos1] * 4, axis=1)
    sa4 = jnp.concatenate([sa1] * 4, axis=1)
    sb4 = jnp.concatenate([sb1] * 4, axis=1)

    def rope(z, c, sa, sb):
        w = z.shape[1]
        return z * c + pltpu.roll(z, ROT_DIM // 2, 1) * sa + pltpu.roll(z, w - ROT_DIM // 2, 1) * sb

    W = W_HEADS
    qa = rope(head_norm(mm(w1_ref, 0, W), gqa_ref), cos4, sa4, sb4)
    qa_o[...] = (qa * ATTN_SCALE).astype(BF16)
    ka = rope(head_norm(mm(w1_ref, W, 2 * W), gka_ref), cos4, sa4, sb4)
    ka_o[...] = ka
    kab_o[...] = ka.astype(BF16)
    va_o[...] = mm(w1_ref, 2 * W, 3 * W)
    vt = lax.dot_general(wvt_ref[...], xn, _NT, preferred_element_type=F32).astype(BF16)
    for pr in range(N_HEADS // 2):
        vat_o[0, pr * PAIR_ROWS:pr * PAIR_ROWS + LANES, :] = vt[pr * LANES:(pr + 1) * LANES]
        vat_o[0, pr * PAIR_ROWS + LANES:(pr + 1) * PAIR_ROWS, :] = jnp.ones((ONES_ROWS, vt.shape[1]), BF16)
    qi_o[...] = rope(mm(w1_ref, 3 * W, 4 * W), cos4, sa4, sb4).astype(BF16)

    z2 = mm(w2_ref, 0, LANES)
    lane = lax.broadcasted_iota(I32, z2.shape, 1)
    is_ki = lane < D_IDX
    ms_ki = jnp.sum(jnp.where(is_ki, z2 * z2, 0.0), axis=-1, keepdims=True) * (1.0 / D_IDX)
    ki = rope(z2 * lax.rsqrt(ms_ki + EPS) * gki_ref[...], cos1, sa1, sb1)
    kiwi = jnp.where(is_ki, ki, z2)
    kiwi_o[...] = kiwi
    kib_o[...] = kiwi[:, :D_IDX].astype(BF16)

    qb = head_norm(mm(w3_ref, 0, W), gqb_ref)
    qb_o[...] = (qb * ATTN_SCALE).astype(BF16)
    kb = head_norm(mm(w3_ref, W, 2 * W), gkb_ref)
    kb_o[...] = kb
    kbb_o[...] = kb.astype(BF16)
    vb = mm(w3_ref, 2 * W, 3 * W)
    vb_o[...] = vb
    vbb_o[...] = vb.astype(BF16)

    d2 = w4_ref.shape[1]
    for c in range(d2 // W):
        zg = mm(w4_ref, c * W, (c + 1) * W)
        gate_o[:, c * W:(c + 1) * W] = 1.0 / (1.0 + jnp.exp(-zg))


def _project(x2d, tables, pw, tm):
    n, d = x2d.shape
    ttab = tables[0].shape[0]
    nt = ttab // tm
    W = W_HEADS
    row = lambda i: (i, 0)
    tab = lambda i: (i % nt, 0)
    d2 = pw['w4'].shape[1]
    in_specs = [
        pl.BlockSpec((tm, d), row),
        _const_spec((1, d)),
        _const_spec(pw['w1'].shape), _const_spec(pw['w2'].shape),
        _const_spec(pw['w3'].shape), _const_spec(pw['w4'].shape),
        _const_spec(pw['wvt'].shape), _const_spec((W, W)),
        _const_spec((1, W)), _const_spec((1, W)), _const_spec((1, LANES)),
        _const_spec((1, W)), _const_spec((1, W)),
        pl.BlockSpec((tm, LANES), tab), pl.BlockSpec((tm, LANES), tab), pl.BlockSpec((tm, LANES), tab),
    ]
    outs = [
        ('qa', W, BF16), ('ka', W, F32), ('ka_bf', W, BF16), ('va', W, F32), ('va_t', None, BF16),
        ('qi', W, BF16), ('kiwi', LANES, F32), ('ki_bf', D_IDX, BF16),
        ('qb', W, BF16), ('kb', W, F32), ('kb_bf', W, BF16), ('vb', W, F32), ('vb_bf', W, BF16),
        ('gate', d2, F32),
    ]

    def out_spec(w):
        if w is None:
            return pl.BlockSpec((1, VT_ROWS, tm), lambda i: (i, 0, 0))
        return pl.BlockSpec((tm, w), row)

    def out_shape(w, dt):
        if w is None:
            return jax.ShapeDtypeStruct((n // tm, VT_ROWS, tm), dt)
        return jax.ShapeDtypeStruct((n, w), dt)

    res = pl.pallas_call(
        _proj_kernel,
        grid=(n // tm,),
        in_specs=in_specs,
        out_specs=[out_spec(w) for _, w, _ in outs],
        out_shape=[out_shape(w, dt) for _, w, dt in outs],
        compiler_params=pltpu.CompilerParams(
            dimension_semantics=("arbitrary",), vmem_limit_bytes=VMEM_LIMIT),
        name="proj",
    )(x2d, pw['gmix'], pw['w1'], pw['w2'], pw['w3'], pw['w4'], pw['wvt'], pw['seg'],
      pw['gqa'], pw['gka'], pw['gki'], pw['gqb'], pw['gkb'], *tables)
    return {name: r for (name, _, _), r in zip(outs, res)}


def _dsa_kernel(bound_ref, qa_ref, qi_ref, kiwi_ref, kidx_ref, ka_ref, vt_ref, o_ref,
                ikey_ref, qis_ref, qpad_ref, wb_ref, m_ref, l_ref, acc_ref,
                *, tq, q_period, kb_size, s_valid, q_pos0, topk):
    H = N_HEADS
    KB = kb_size
    kb_shift = KB.bit_length() - 1
    j = pl.program_id(1)
    q0 = q_pos0 + j * tq
    first_chunk = jnp.right_shift(q0, 6)
    last_chunk = jnp.right_shift(q0 + (q_period - 1), 6)
    n_kb = jnp.right_shift(jnp.minimum((last_chunk + 1) * CHUNK, s_valid) + (KB - 1), kb_shift)
    n_full = jnp.right_shift(jnp.minimum((first_chunk + 1) * CHUNK, s_valid), kb_shift)

    col = lax.broadcasted_iota(I32, (SUBLANES, tq), 1) & (q_period - 1)
    col_chunk = jnp.right_shift(q0 + col, 6)
    n_vis = jnp.minimum((col_chunk + 1) * CHUNK, s_valid)

    qi = qi_ref[0]
    qa = qa_ref[0]
    w_t = jnp.transpose(kiwi_ref[0])
    lane = lax.broadcasted_iota(I32, (tq, LANES), 1)
    for h in range(H):
        qis_ref[h * tq:(h + 1) * tq, :] = qi[:, h * D_IDX:(h + 1) * D_IDX]
        pair = qa[:, (h // 2) * LANES:(h // 2 + 1) * LANES]
        own = (lane >= HEAD_DIM) if (h % 2) else (lane < HEAD_DIM)
        qpad_ref[h] = jnp.where(own, pair, jnp.zeros_like(pair))
        wb_ref[h] = jnp.broadcast_to(w_t[D_IDX + h:D_IDX + h + 1, :], (KB, tq))

    def score_block(kb, kmax, masked):
        k0 = pl.multiple_of(kb * KB, KB)
        kblk = kidx_ref[0, pl.ds(k0, KB), :]
        s = lax.dot_general(kblk, qis_ref[...], _NT, preferred_element_type=F32)
        acc = None
        for h in range(H):
            r = jnp.maximum(s[:, h * tq:(h + 1) * tq], 0.0) * wb_ref[h]
            acc = r if acc is None else acc + r
        b = lax.bitcast_convert_type(acc, I32)
        key = b ^ (jnp.right_shift(b, 31) & INT_MAX)
        if masked:
            kpos = k0 + lax.broadcasted_iota(I32, (KB, tq), 0)
            vis = (jnp.right_shift(kpos, 6) <= col_chunk[:1]) & (kpos < s_valid)
            key = jnp.where(vis, key, INT_MIN)
        ikey_ref[pl.ds(k0, KB), :] = key
        return jnp.maximum(kmax, jnp.max(key, axis=0, keepdims=True))

    kmax = jnp.full((1, tq), INT_MIN, I32)
    kmax = lax.fori_loop(0, n_full, functools.partial(score_block, masked=False), kmax)
    kmax = lax.fori_loop(n_full, n_kb, functools.partial(score_block, masked=True), kmax)
    kmax = jnp.broadcast_to(kmax, (SUBLANES, tq))

    n_acc = 4

    def count_ge(t):
        def body(kb, cnts):
            k0 = pl.multiple_of(kb * KB, KB)
            cnts = list(cnts)
            blk = ikey_ref[pl.ds(k0, KB), :]
            for r in range(KB // SUBLANES):
                rows = blk[r * SUBLANES:(r + 1) * SUBLANES]
                cnts[r % n_acc] = cnts[r % n_acc] + jnp.where(rows >= t, 1, 0)
            return tuple(cnts)
        zero = jnp.zeros((SUBLANES, tq), I32)
        cnts = lax.fori_loop(0, n_kb, body, (zero,) * n_acc)
        tot = cnts[0] + cnts[1] + cnts[2] + cnts[3]
        return jnp.broadcast_to(jnp.sum(tot, axis=0, keepdims=True), (SUBLANES, tq))

    span = 1 << 24
    floor = INT_MIN + 1
    guess = jnp.where(kmax > floor + span, kmax - span, floor)
    c_guess = count_ge(guess)
    ok = c_guess >= topk
    lo0 = jnp.where(ok, guess, floor)
    clo0 = jnp.where(ok, c_guess, n_vis)
    hi0 = jnp.where(ok, jnp.where(kmax == INT_MAX, kmax, kmax + 1), guess)

    def converged(lo, hi, clo):
        return jnp.min(((clo <= topk) | (lo + 1 >= hi)).astype(I32))

    def bisect_cond(carry):
        it, done = carry[0], carry[1]
        return (it < 34) & (done == 0)

    def bisect_body(carry):
        it, _, lo, hi, clo = carry
        mid = jnp.right_shift(lo, 1) + jnp.right_shift(hi, 1) + (lo & hi & 1)
        c = count_ge(mid)
        ge = c >= topk
        lo = jnp.where(ge, mid, lo)
        hi = jnp.where(ge, hi, mid)
        clo = jnp.where(ge, c, clo)
        return it + 1, converged(lo, hi, clo), lo, hi, clo

    _, _, tau, _, clo = lax.while_loop(
        bisect_cond, bisect_body, (jnp.int32(0), converged(lo0, hi0, clo0), lo0, hi0, clo0))
    tau_row = tau[:1]

    @pl.when(jnp.max(clo) > topk)
    def _():
        need = (topk - count_ge(tau + 1))[:1].astype(F32)
        tri = (lax.broadcasted_iota(I32, (KB, KB), 0)
               >= lax.broadcasted_iota(I32, (KB, KB), 1)).astype(BF16)

        def drop_block(kb, run):
            k0 = pl.multiple_of(kb * KB, KB)
            blk = ikey_ref[pl.ds(k0, KB), :]
            eq = blk == tau_row
            pref = jnp.dot(tri, jnp.where(eq, 1.0, 0.0).astype(BF16), preferred_element_type=F32)
            keep = (run + pref) <= need
            ikey_ref[pl.ds(k0, KB), :] = jnp.where(eq & jnp.logical_not(keep), INT_MIN, blk)
            return run + pref[KB - 1:KB, :]

        lax.fori_loop(0, n_kb, drop_block, jnp.zeros((1, tq), F32))

    neg_bound = -bound_ref[0, 0]
    acc_ref[...] = jnp.zeros(acc_ref.shape, F32)

    def attend_block(kb, carry):
        k0 = pl.multiple_of(kb * KB, KB)
        mb = jnp.where(ikey_ref[pl.ds(k0, KB), :] >= tau_row, neg_bound, NEG)
        logits = [lax.dot_general(ka_ref[0, pl.ds(k0, KB), (h // 2) * LANES:(h // 2 + 1) * LANES],
                                  qpad_ref[h], _NT, preferred_element_type=F32) for h in range(H)]
        probs = [jnp.exp(s + mb).astype(BF16) for s in logits]
        for h in range(H):
            pr = h // 2
            acc_ref[h] += jnp.dot(vt_ref[0, kb, pr * PAIR_ROWS:(pr + 1) * PAIR_ROWS, :], probs[h],
                                  preferred_element_type=F32)
        return carry

    lax.fori_loop(0, n_kb, attend_block, 0)
    for h in range(H):
        l_ref[h] = acc_ref[h, LANES:LANES + 1, :]
    l_min = jnp.min(l_ref[...])

    @pl.when(jnp.logical_not(l_min >= L_FLOOR))
    def _():
        m_ref[...] = jnp.full(m_ref.shape, NEG, F32)
        l_ref[...] = jnp.zeros(l_ref.shape, F32)
        acc_ref[...] = jnp.zeros(acc_ref.shape, F32)

        def attend_exact(kb, carry):
            k0 = pl.multiple_of(kb * KB, KB)
            mb = jnp.where(ikey_ref[pl.ds(k0, KB), :] >= tau_row, 0.0, NEG)
            for h in range(H):
                pr = h // 2
                kblk = ka_ref[0, pl.ds(k0, KB), pr * LANES:(pr + 1) * LANES]
                s = lax.dot_general(kblk, qpad_ref[h], _NT, preferred_element_type=F32) + mb
                m_prev = m_ref[h]
                m_new = jnp.maximum(m_prev, jnp.max(s, axis=0, keepdims=True))
                alpha = jnp.exp(m_prev - m_new)
                p = jnp.exp(s - m_new)
                l_ref[h] = alpha * l_ref[h] + jnp.sum(p, axis=0, keepdims=True)
                pv = jnp.dot(vt_ref[0, kb, pr * PAIR_ROWS:pr * PAIR_ROWS + LANES, :], p.astype(BF16),
                             preferred_element_type=F32)
                acc_ref[h, :LANES, :] = alpha * acc_ref[h, :LANES, :] + pv
                m_ref[h] = m_new
            return carry

        lax.fori_loop(0, n_kb, attend_exact, 0)

    row = lax.broadcasted_iota(I32, (LANES, tq), 0)
    for pr in range(H // 2):
        even = acc_ref[2 * pr, :LANES, :] / l_ref[2 * pr]
        odd = acc_ref[2 * pr + 1, :LANES, :] / l_ref[2 * pr + 1]
        out_t = jnp.where(row < HEAD_DIM, even, odd)
        o_ref[0, :, pr * LANES:(pr + 1) * LANES] = jnp.transpose(out_t).astype(BF16)


def _dsa(bound, qa, qi, kiwi, kidx, ka, vt, *, tq, q_period, s_valid, q_pos0, topk):
    b, t, w = qa.shape
    s_pad = kidx.shape[1]
    kb_size = vt.shape[3]
    assert t % tq == 0 and s_pad == vt.shape[1] * kb_size and tq % LANES == 0
    assert kb_size & (kb_size - 1) == 0 and q_period & (q_period - 1) == 0
    qspec = lambda width: pl.BlockSpec((1, tq, width), lambda bi, ji: (bi, ji, 0))
    kspec = lambda *dims: pl.BlockSpec((1,) + dims, lambda bi, ji: (bi,) + (0,) * len(dims),
                                       pipeline_mode=pl.Buffered(1))
    kern = functools.partial(_dsa_kernel, tq=tq, q_period=q_period, kb_size=kb_size, s_valid=s_valid,
                             q_pos0=q_pos0, topk=topk)
    return pl.pallas_call(
        kern,
        grid=(b, t // tq),
        in_specs=[pl.BlockSpec(memory_space=pltpu.SMEM),
                  qspec(w), qspec(w), qspec(LANES), kspec(s_pad, D_IDX), kspec(s_pad, w),
                  kspec(s_pad // kb_size, VT_ROWS, kb_size)],
        out_specs=qspec(w),
        out_shape=jax.ShapeDtypeStruct((b, t, w), BF16),
        scratch_shapes=[
            pltpu.VMEM((s_pad, tq), I32),
            pltpu.VMEM((N_HEADS * tq, D_IDX), BF16),
            pltpu.VMEM((N_HEADS, tq, LANES), BF16),
            pltpu.VMEM((N_HEADS, kb_size, tq), F32),
            pltpu.VMEM((N_HEADS, 1, tq), F32),
            pltpu.VMEM((N_HEADS, 1, tq), F32),
            pltpu.VMEM((N_HEADS, PAIR_ROWS, tq), F32),
        ],
        compiler_params=pltpu.CompilerParams(
            dimension_semantics=("arbitrary", "arbitrary"), vmem_limit_bytes=VMEM_LIMIT),
        name="dsa",
    )(bound, qa, qi, kiwi, kidx, ka, vt)


def _band_kernel(*refs, tq, kb_size, nkb):
    q_ref = refs[0]
    k_refs = refs[1:1 + nkb]
    v_refs = refs[1 + nkb:1 + 2 * nkb]
    bias_ref = refs[1 + 2 * nkb]
    o_ref = refs[2 + 2 * nkb]
    j = pl.program_id(1)
    q = q_ref[0]
    lane = lax.broadcasted_iota(I32, (tq, LANES), 1)
    for pr in range(N_HEADS // 2):
        p0 = pr * LANES
        qpair = q[:, p0:p0 + LANES]
        outs = []
        for par in range(2):
            h = 2 * pr + par
            own = (lane >= HEAD_DIM) if par else (lane < HEAD_DIM)
            qp = jnp.where(own, qpair, jnp.zeros_like(qpair))
            parts = []
            for i in range(nkb):
                s = lax.dot_general(qp, k_refs[i][0, :, p0:p0 + LANES], _NT, preferred_element_type=F32)
                s = s + bias_ref[h, :, i * kb_size:(i + 1) * kb_size]
                if i < nkb - 1:
                    s = jnp.where(j + (i - (nkb - 1)) >= 0, s, NEG)
                parts.append(s)
            m = parts[0].max(axis=1, keepdims=True)
            for s in parts[1:]:
                m = jnp.maximum(m, s.max(axis=1, keepdims=True))
            l = jnp.zeros((tq, 1), F32)
            o = jnp.zeros((tq, LANES), F32)
            for i in range(nkb):
                p = jnp.exp(parts[i] - m)
                l = l + jnp.sum(p, axis=1, keepdims=True)
                o = o + jnp.dot(p.astype(BF16), v_refs[i][0, :, p0:p0 + LANES], preferred_element_type=F32)
            outs.append(o / l)
        o_ref[0, :, p0:p0 + LANES] = jnp.where(lane < HEAD_DIM, outs[0], outs[1]).astype(BF16)


def _band(q, k, v, bias, *, tq, kb_size, nkb):
    b, t, w = q.shape
    qspec = pl.BlockSpec((1, tq, w), lambda bi, ji: (bi, ji, 0))

    def kspec(i):
        off = i - (nkb - 1)
        return pl.BlockSpec((1, kb_size, w), lambda bi, ji: (bi, jnp.maximum(ji + off, 0), 0))

    kern = functools.partial(_band_kernel, tq=tq, kb_size=kb_size, nkb=nkb)
    return pl.pallas_call(
        kern,
        grid=(b, t // tq),
        in_specs=[qspec] + [kspec(i) for i in range(nkb)] * 2 + [_const_spec(bias.shape)],
        out_specs=qspec,
        out_shape=jax.ShapeDtypeStruct((b, t, w), BF16),
        compiler_params=pltpu.CompilerParams(
            dimension_semantics=("arbitrary", "arbitrary"), vmem_limit_bytes=VMEM_LIMIT),
        name="band",
    )(q, *([k] * nkb), *([v] * nkb), bias)


def _band_bias(table, tq, nk):
    n_rel = table.shape[0]
    pad = 2 * (tq + nk)
    ext = jnp.pad(table[::-1].astype(F32), ((pad, pad), (0, 0)), mode='edge')
    length = tq + nk
    start = REL_CLIP + pad - (BAND_ROWS + tq - 1)
    g = lax.slice_in_dim(ext, start, start + length, axis=0)
    assert start >= 0 and start + length <= n_rel + 2 * pad
    rows = jnp.tile(g, (tq, 1))[:tq * (length - 1)].reshape(tq, length - 1, -1)
    b = rows[:, tq - 1:tq - 1 + nk, :]
    r = jnp.arange(tq)[:, None]
    c = jnp.arange(nk)[None, :]
    qc, kc = r // CHUNK, c // CHUNK
    ok = (kc >= qc) & (kc <= qc + BAND_CHUNKS)
    return jnp.where(ok[None], jnp.moveaxis(b, -1, 0), NEG)


def _ffn_kernel(x_ref, oa_ref, ob_ref, gate_ref, woa_ref, wob_ref, wout_ref, gffn_ref,
                wup_ref, wdn_ref, y_ref, *, ff_chunk):
    d = x_ref.shape[1]
    a = jnp.dot(oa_ref[...], woa_ref[...], preferred_element_type=F32)
    b = jnp.dot(ob_ref[...], wob_ref[...], preferred_element_type=F32)
    m = gate_ref[:, :d] * a + gate_ref[:, d:] * b
    x1 = x_ref[...] + jnp.dot(m.astype(BF16), wout_ref[...], preferred_element_type=F32)
    ms = jnp.mean(x1 * x1, axis=-1, keepdims=True)
    xn = (x1 * lax.rsqrt(ms + EPS) * gffn_ref[...]).astype(BF16)
    acc = x1
    for c in range(wup_ref.shape[1] // ff_chunk):
        h = jnp.dot(xn, wup_ref[:, c * ff_chunk:(c + 1) * ff_chunk], preferred_element_type=F32)
        h = jnp.square(jnp.maximum(h, 0.0))
        acc = acc + jnp.dot(h.astype(BF16), wdn_ref[c * ff_chunk:(c + 1) * ff_chunk, :],
                            preferred_element_type=F32)
    y_ref[...] = acc


def _ffn(x2d, oa, ob, gate, fw, tm):
    n, d = x2d.shape
    w = oa.shape[1]
    row = lambda i: (i, 0)
    kern = functools.partial(_ffn_kernel, ff_chunk=min(1024, fw['wup'].shape[1]))
    return pl.pallas_call(
        kern,
        grid=(n // tm,),
        in_specs=[pl.BlockSpec((tm, d), row), pl.BlockSpec((tm, w), row), pl.BlockSpec((tm, w), row),
                  pl.BlockSpec((tm, 2 * d), row),
                  _const_spec(fw['woa'].shape), _const_spec(fw['wob'].shape), _const_spec(fw['wout'].shape),
                  _const_spec((1, d)), _const_spec(fw['wup'].shape), _const_spec(fw['wdn'].shape)],
        out_specs=pl.BlockSpec((tm, d), row),
        out_shape=jax.ShapeDtypeStruct((n, d), F32),
        compiler_params=pltpu.CompilerParams(
            dimension_semantics=("arbitrary",), vmem_limit_bytes=VMEM_LIMIT),
        name="ffn",
    )(x2d, oa, ob, gate, fw['woa'], fw['wob'], fw['wout'], fw['gffn'], fw['wup'], fw['wdn'])


def _rope_tables(pos):
    half = ROT_DIM // 2
    inv = ROPE_THETA ** (-jnp.arange(half, dtype=F32) / half)
    ang = pos.astype(F32)[:, None] * inv[None, :]
    cos, sin = jnp.cos(ang), jnp.sin(ang)
    t = pos.shape[0]
    rest = HEAD_DIM - ROT_DIM
    c = jnp.concatenate([cos, cos, jnp.ones((t, rest), F32)], axis=1)
    sa = jnp.concatenate([jnp.zeros((t, half), F32), sin, jnp.zeros((t, rest), F32)], axis=1)
    sb = jnp.concatenate([-sin, jnp.zeros((t, half + rest), F32)], axis=1)
    rep = LANES // HEAD_DIM
    return tuple(jnp.tile(a, (1, rep)) for a in (c, sa, sb))


def _tile_gain(g):
    return jnp.tile(g.astype(F32), N_HEADS)[None, :]


def _pick_tile(n, pref):
    t = min(n, pref)
    while n % t:
        t //= 2
    return t


def kernel(x_prompt, x_sample, cache_k_a, cache_v_a, cache_k_idx, cache_k_b, cache_v_b, norm_mix, w_in,
           qnorm_a, knorm_a, knorm_idx, qnorm_b, knorm_b, rel_bias_b, w_o_a, w_o_b, w_out, norm_ffn,
           w_up, w_down):
    B, T, D = x_prompt.shape
    Bs, Ts, _ = x_sample.shape
    depth = w_in.shape[0]
    past = cache_k_a.shape[2]
    rows_b = cache_k_b.shape[2]
    W = W_HEADS
    H = N_HEADS
    assert T % CHUNK == 0 and Ts % CHUNK == 0 and past % CHUNK == 0
    assert rows_b == BAND_ROWS and T >= BAND_ROWS
    topk_p = min(TOPK_MAX, T // 4)
    topk_s = min(TOPK_MAX, (past + Ts) // 4)
    keep_p = min(BAND_ROWS, T)

    pos_p = jnp.arange(T, dtype=jnp.int32)
    pos_s = past + jnp.arange(Ts, dtype=jnp.int32)
    tab_p = _rope_tables(pos_p)
    tab_s = tuple(jnp.tile(t, (Bs, 1)) for t in _rope_tables(pos_s))

    seg = jnp.where((jnp.arange(W)[:, None] // HEAD_DIM) == (jnp.arange(W)[None, :] // HEAD_DIM),
                    1.0 / HEAD_DIM, 0.0).astype(BF16)

    xp = x_prompt.reshape(B * T, D)
    xs = x_sample.reshape(Bs * Ts, D)
    outs_p = [[] for _ in range(5)]
    outs_s = [[] for _ in range(5)]

    tm_p = _pick_tile(T, 256)
    tq_p = _pick_tile(T, 256)
    band_tq = _pick_tile(T, 256)
    assert BAND_ROWS % band_tq == 0 and tq_p % LANES == 0

    for l in range(depth):
        wl = w_in[l].astype(BF16)
        o_ki = 4 * W
        o_b = o_ki + D_IDX + H
        o_g = o_b + 3 * W
        w2 = jnp.pad(wl[:, o_ki:o_b], ((0, 0), (0, LANES - D_IDX - H)))
        pw = dict(
            gmix=norm_mix[l][None, :].astype(F32),
            w1=wl[:, :o_ki], w2=w2, w3=wl[:, o_b:o_g], w4=wl[:, o_g:], wvt=wl[:, 2 * W:3 * W].T, seg=seg,
            gqa=_tile_gain(qnorm_a[l]), gka=_tile_gain(knorm_a[l]),
            gki=jnp.pad(knorm_idx[l].astype(F32), (0, LANES - D_IDX))[None, :],
            gqb=_tile_gain(qnorm_b[l]), gkb=_tile_gain(knorm_b[l]),
        )
        fw = dict(woa=w_o_a[l].astype(BF16), wob=w_o_b[l].astype(BF16), wout=w_out[l].astype(BF16),
                  gffn=norm_ffn[l][None, :].astype(F32), wup=w_up[l].astype(BF16),
                  wdn=w_down[l].astype(BF16))
        table = rel_bias_b[l]

        pp = _project(xp, tab_p, pw, tm_p)
        r3 = lambda a: a.reshape(B, T, a.shape[-1])
        bound_a = (1.01 * HEAD_DIM * ATTN_SCALE * jnp.max(jnp.abs(qnorm_a[l])) * jnp.max(jnp.abs(knorm_a[l]))
                   ).astype(F32).reshape(1, 1)
        oa = _dsa(bound_a, r3(pp['qa']), r3(pp['qi']), r3(pp['kiwi']), r3(pp['ki_bf']), r3(pp['ka_bf']),
                  pp['va_t'].reshape(B, T // tm_p, VT_ROWS, tm_p),
                  tq=tq_p, q_period=tq_p, s_valid=T, q_pos0=0, topk=topk_p)
        nkb = BAND_ROWS // band_tq + 1
        ob = _band(r3(pp['qb']), r3(pp['kb_bf']), r3(pp['vb_bf']),
                   _band_bias(table, band_tq, nkb * band_tq), tq=band_tq, kb_size=band_tq, nkb=nkb)
        xp = _ffn(xp, oa.reshape(B * T, W), ob.reshape(B * T, W), pp['gate'], fw, _pick_tile(B * T, 256))
        outs_p[0].append(pp['ka'].reshape(B, T, H, HEAD_DIM))
        outs_p[1].append(pp['va'].reshape(B, T, H, HEAD_DIM))
        outs_p[2].append(r3(pp['kiwi'])[:, :, :D_IDX])
        outs_p[3].append(pp['kb'].reshape(B, T, H, HEAD_DIM)[:, T - keep_p:])
        outs_p[4].append(pp['vb'].reshape(B, T, H, HEAD_DIM)[:, T - keep_p:])

        ps = _project(xs, tab_s, pw, _pick_tile(Bs * Ts, 256))
        s3 = lambda a: a.reshape(Bs, Ts, a.shape[-1])
        s_all = past + Ts
        kb_s = LANES
        s_pad = -(-s_all // kb_s) * kb_s
        tq_s = -(-Ts // LANES) * LANES
        assert tq_s % Ts == 0 and Ts & (Ts - 1) == 0

        def with_cache(cache, new, width):
            full = jnp.concatenate([cache.reshape(Bs, -1, width).astype(BF16), s3(new)], axis=1)
            return jnp.pad(full, ((0, 0), (0, s_pad - s_all), (0, 0)))

        rep_q = lambda a: jnp.tile(s3(a), (1, tq_s // Ts, 1))
        v_all = with_cache(cache_v_a[l], ps['va'].astype(BF16), W)
        n_kb_s = s_pad // kb_s
        v_t = v_all.reshape(Bs, n_kb_s, kb_s, H // 2, LANES).transpose(0, 1, 3, 4, 2)
        v_t = jnp.concatenate([v_t, jnp.ones((Bs, n_kb_s, H // 2, ONES_ROWS, kb_s), BF16)], axis=3)
        v_t = v_t.reshape(Bs, n_kb_s, VT_ROWS, kb_s)
        cache_norm = jnp.sqrt(jnp.max(jnp.sum(jnp.square(cache_k_a[l].astype(F32)), axis=-1)))
        bound_s = jnp.maximum(bound_a, (1.01 * HEAD_DIM ** 0.5 * ATTN_SCALE) * jnp.max(jnp.abs(qnorm_a[l]))
                              * cache_norm).astype(F32).reshape(1, 1)
        oa = _dsa(bound_s, rep_q(ps['qa']), rep_q(ps['qi']), rep_q(ps['kiwi']),
                  with_cache(cache_k_idx[l], ps['ki_bf'], D_IDX),
                  with_cache(cache_k_a[l], ps['ka_bf'], W), v_t,
                  tq=tq_s, q_period=Ts, s_valid=s_all, q_pos0=past, topk=topk_s)[:, :Ts]
        kb_all = jnp.concatenate([cache_k_b[l].reshape(Bs, rows_b, W).astype(BF16), s3(ps['kb_bf'])], axis=1)
        vb_all = jnp.concatenate([cache_v_b[l].reshape(Bs, rows_b, W).astype(BF16), s3(ps['vb_bf'])], axis=1)
        ob = _band(s3(ps['qb']), kb_all, vb_all, _band_bias(table, Ts, rows_b + Ts),
                   tq=Ts, kb_size=rows_b + Ts, nkb=1)
        xs = _ffn(xs, oa.reshape(Bs * Ts, W), ob.reshape(Bs * Ts, W), ps['gate'], fw,
                  _pick_tile(Bs * Ts, 256))
        outs_s[0].append(ps['ka'].reshape(Bs, Ts, H, HEAD_DIM))
        outs_s[1].append(ps['va'].reshape(Bs, Ts, H, HEAD_DIM))
        outs_s[2].append(s3(ps['kiwi'])[:, :, :D_IDX])
        outs_s[3].append(ps['kb'].reshape(Bs, Ts, H, HEAD_DIM))
        outs_s[4].append(ps['vb'].reshape(Bs, Ts, H, HEAD_DIM))

    return (xp.reshape(B, T, D), xs.reshape(Bs, Ts, D),
            *[jnp.stack(o) for o in outs_p], *[jnp.stack(o) for o in outs_s])
```

```python
import functools

import jax
import jax.numpy as jnp
from jax import lax
from jax.experimental import pallas as pl
from jax.experimental.pallas import tpu as pltpu

CHUNK = 64
HEAD_DIM = 64
N_HEADS = 8
D_IDX = 64
ROT_DIM = HEAD_DIM // 4
ROPE_THETA = 500000.0
TOPK_MAX = 256
BAND_CHUNKS = 8
BAND_ROWS = BAND_CHUNKS * CHUNK
REL_CLIP = 128
EPS = 1e-6
ATTN_SCALE = HEAD_DIM ** -0.5
W_HEADS = N_HEADS * HEAD_DIM

LANES = 128
SUBLANES = 8
ONES_ROWS = 16
PAIR_ROWS = LANES + ONES_ROWS
VT_ROWS = (N_HEADS // 2) * PAIR_ROWS
L_FLOOR = 1e-30
PASSES_PER_CHECK = 3
NEG = -1e30
INT_MIN = -2 ** 31
INT_MAX = 2 ** 31 - 1
VMEM_LIMIT = 56 * 1024 * 1024

F32 = jnp.float32
BF16 = jnp.bfloat16
I32 = jnp.int32

_NT = (((1,), (1,)), ((), ()))


def _float_to_key(x):
    b = lax.bitcast_convert_type(x, I32)
    sgn = jnp.right_shift(b, 31)
    return (b ^ (sgn & INT_MAX)) - sgn


def _key_to_float(k):
    sgn = jnp.right_shift(k, 31)
    return lax.bitcast_convert_type((k + sgn) ^ (sgn & INT_MAX), F32)


def _const_spec(shape):
    nd = len(shape)
    return pl.BlockSpec(shape, lambda *_: (0,) * nd, pipeline_mode=pl.Buffered(1))


def _proj_kernel(x_ref, gmix_ref, w1_ref, w2_ref, w3_ref, w4_ref, wvt_ref, seg_ref,
                 gqa_ref, gka_ref, gki_ref, gqb_ref, gkb_ref,
                 cos_ref, s1_ref, s2_ref,
                 qa_o, ka_o, kab_o, va_o, vat_o, qi_o, kiwi_o, kib_o,
                 qb_o, kb_o, kbb_o, vb_o, vbb_o, gate_o):
    x = x_ref[...]
    ms = jnp.mean(x * x, axis=-1, keepdims=True)
    xn = (x * lax.rsqrt(ms + EPS) * gmix_ref[...]).astype(BF16)

    def mm(w_ref, lo, hi):
        return jnp.dot(xn, w_ref[:, lo:hi], preferred_element_type=F32)

    def head_norm(z, gain_ref):
        msq = jnp.dot((z * z).astype(BF16), seg_ref[...], preferred_element_type=F32)
        return z * lax.rsqrt(msq + EPS) * gain_ref[...]

    cos1, sa1, sb1 = cos_ref[...], s1_ref[...], s2_ref[...]
    cos4 = jnp.concatenate([cos1] * 4, axis=1)
    sa4 = jnp.concatenate([sa1] * 4, axis=1)
    sb4 = jnp.concatenate([sb1] * 4, axis=1)

    def rope(z, c, sa, sb):
        w = z.shape[1]
        return z * c + pltpu.roll(z, ROT_DIM // 2, 1) * sa + pltpu.roll(z, w - ROT_DIM // 2, 1) * sb

    W = W_HEADS
    qa = rope(head_norm(mm(w1_ref, 0, W), gqa_ref), cos4, sa4, sb4)
    qa_o[...] = (qa * ATTN_SCALE).astype(BF16)
    ka = rope(head_norm(mm(w1_ref, W, 2 * W), gka_ref), cos4, sa4, sb4)
    ka_o[...] = ka
    kab_o[...] = ka.astype(BF16)
    va_o[...] = mm(w1_ref, 2 * W, 3 * W)
    vt = lax.dot_general(wvt_ref[...], xn, _NT, preferred_element_type=F32).astype(BF16)
    for pr in range(N_HEADS // 2):
        vat_o[0, pr * PAIR_ROWS:pr * PAIR_ROWS + LANES, :] = vt[pr * LANES:(pr + 1) * LANES]
        vat_o[0, pr * PAIR_ROWS + LANES:(pr + 1) * PAIR_ROWS, :] = jnp.ones((ONES_ROWS, vt.shape[1]), BF16)
    qi_o[...] = rope(mm(w1_ref, 3 * W, 4 * W), cos4, sa4, sb4).astype(BF16)

    z2 = mm(w2_ref, 0, LANES)
    lane = lax.broadcasted_iota(I32, z2.shape, 1)
    is_ki = lane < D_IDX
    ms_ki = jnp.sum(jnp.where(is_ki, z2 * z2, 0.0), axis=-1, keepdims=True) * (1.0 / D_IDX)
    ki = rope(z2 * lax.rsqrt(ms_ki + EPS) * gki_ref[...], cos1, sa1, sb1)
    kiwi = jnp.where(is_ki, ki, z2)
    kiwi_o[...] = kiwi
    kib_o[...] = kiwi[:, :D_IDX].astype(BF16)

    qb = head_norm(mm(w3_ref, 0, W), gqb_ref)
    qb_o[...] = (qb * ATTN_SCALE).astype(BF16)
    kb = head_norm(mm(w3_ref, W, 2 * W), gkb_ref)
    kb_o[...] = kb
    kbb_o[...] = kb.astype(BF16)
    vb = mm(w3_ref, 2 * W, 3 * W)
    vb_o[...] = vb
    vbb_o[...] = vb.astype(BF16)

    d2 = w4_ref.shape[1]
    for c in range(d2 // W):
        zg = mm(w4_ref, c * W, (c + 1) * W)
        gate_o[:, c * W:(c + 1) * W] = 1.0 / (1.0 + jnp.exp(-zg))


def _project(x2d, tables, pw, tm):
    n, d = x2d.shape
    ttab = tables[0].shape[0]
    nt = ttab // tm
    W = W_HEADS
    row = lambda i: (i, 0)
    tab = lambda i: (i % nt, 0)
    d2 = pw['w4'].shape[1]
    in_specs = [
        pl.BlockSpec((tm, d), row),
        _const_spec((1, d)),
        _const_spec(pw['w1'].shape), _const_spec(pw['w2'].shape),
        _const_spec(pw['w3'].shape), _const_spec(pw['w4'].shape),
        _const_spec(pw['wvt'].shape), _const_spec((W, W)),
        _const_spec((1, W)), _const_spec((1, W)), _const_spec((1, LANES)),
        _const_spec((1, W)), _const_spec((1, W)),
        pl.BlockSpec((tm, LANES), tab), pl.BlockSpec((tm, LANES), tab), pl.BlockSpec((tm, LANES), tab),
    ]
    outs = [
        ('qa', W, BF16), ('ka', W, F32), ('ka_bf', W, BF16), ('va', W, F32), ('va_t', None, BF16),
        ('qi', W, BF16), ('kiwi', LANES, F32), ('ki_bf', D_IDX, BF16),
        ('qb', W, BF16), ('kb', W, F32), ('kb_bf', W, BF16), ('vb', W, F32), ('vb_bf', W, BF16),
        ('gate', d2, F32),
    ]

    def out_spec(w):
        if w is None:
            return pl.BlockSpec((1, VT_ROWS, tm), lambda i: (i, 0, 0))
        return pl.BlockSpec((tm, w), row)

    def out_shape(w, dt):
        if w is None:
            return jax.ShapeDtypeStruct((n // tm, VT_ROWS, tm), dt)
        return jax.ShapeDtypeStruct((n, w), dt)

    res = pl.pallas_call(
        _proj_kernel,
        grid=(n // tm,),
        in_specs=in_specs,
        out_specs=[out_spec(w) for _, w, _ in outs],
        out_shape=[out_shape(w, dt) for _, w, dt in outs],
        compiler_params=pltpu.CompilerParams(
            dimension_semantics=("arbitrary",), vmem_limit_bytes=VMEM_LIMIT),
        name="proj",
    )(x2d, pw['gmix'], pw['w1'], pw['w2'], pw['w3'], pw['w4'], pw['wvt'], pw['seg'],
      pw['gqa'], pw['gka'], pw['gki'], pw['gqb'], pw['gkb'], *tables)
    return {name: r for (name, _, _), r in zip(outs, res)}


def _dsa_kernel(bound_ref, qa_ref, qi_ref, kiwi_ref, kidx_ref, ka_ref, vt_ref, o_ref,
                ikey_ref, qis_ref, qpad_ref, wb_ref, m_ref, l_ref, acc_ref,
                *, tq, q_period, kb_size, s_valid, q_pos0, topk):
    H = N_HEADS
    KB = kb_size
    kb_shift = KB.bit_length() - 1
    j = pl.program_id(1)
    q0 = q_pos0 + j * tq
    first_chunk = jnp.right_shift(q0, 6)
    last_chunk = jnp.right_shift(q0 + (q_period - 1), 6)
    n_kb = jnp.right_shift(jnp.minimum((last_chunk + 1) * CHUNK, s_valid) + (KB - 1), kb_shift)
    n_full = jnp.right_shift(jnp.minimum((first_chunk + 1) * CHUNK, s_valid), kb_shift)

    col = lax.broadcasted_iota(I32, (SUBLANES, tq), 1) & (q_period - 1)
    col_chunk = jnp.right_shift(q0 + col, 6)
    n_vis = jnp.minimum((col_chunk + 1) * CHUNK, s_valid)

    qi = qi_ref[0]
    qa = qa_ref[0]
    w_t = jnp.transpose(kiwi_ref[0])
    lane = lax.broadcasted_iota(I32, (tq, LANES), 1)
    for h in range(H):
        qis_ref[h * tq:(h + 1) * tq, :] = qi[:, h * D_IDX:(h + 1) * D_IDX]
        pair = qa[:, (h // 2) * LANES:(h // 2 + 1) * LANES]
        own = (lane >= HEAD_DIM) if (h % 2) else (lane < HEAD_DIM)
        qpad_ref[h] = jnp.where(own, pair, jnp.zeros_like(pair))
        wb_ref[h] = jnp.broadcast_to(w_t[D_IDX + h:D_IDX + h + 1, :], (KB, tq))

    def score_block(kb, stats, masked):
        kmax, kmin, c_pos, c_nn = stats
        k0 = pl.multiple_of(kb * KB, KB)
        kblk = kidx_ref[0, pl.ds(k0, KB), :]
        s = lax.dot_general(kblk, qis_ref[...], _NT, preferred_element_type=F32)
        acc = None
        for h in range(H):
            r = jnp.maximum(s[:, h * tq:(h + 1) * tq], 0.0) * wb_ref[h]
            acc = r if acc is None else acc + r
        key = _float_to_key(acc)
        key_lo = key
        if masked:
            kpos = k0 + lax.broadcasted_iota(I32, (KB, tq), 0)
            vis = (jnp.right_shift(kpos, 6) <= col_chunk[:1]) & (kpos < s_valid)
            key_lo = jnp.where(vis, key, INT_MAX)
            key = jnp.where(vis, key, INT_MIN)
        ikey_ref[pl.ds(k0, KB), :] = key
        return (jnp.maximum(kmax, jnp.max(key, axis=0, keepdims=True)),
                jnp.minimum(kmin, jnp.min(key_lo, axis=0, keepdims=True)),
                c_pos + jnp.sum(jnp.where(key >= 1, 1, 0), axis=0, keepdims=True),
                c_nn + jnp.sum(jnp.where(key >= 0, 1, 0), axis=0, keepdims=True))

    stats = (jnp.full((1, tq), INT_MIN, I32), jnp.full((1, tq), INT_MAX, I32),
             jnp.zeros((1, tq), I32), jnp.zeros((1, tq), I32))
    stats = lax.fori_loop(0, n_full, functools.partial(score_block, masked=False), stats)
    stats = lax.fori_loop(n_full, n_kb, functools.partial(score_block, masked=True), stats)

    n_acc = 4
    floor = INT_MIN + 1
    span = 1 << 24

    def search(g):
        lanes = slice(g * LANES, (g + 1) * LANES)
        rep = lambda a: jnp.broadcast_to(a[:, lanes], (SUBLANES, LANES))
        kmax, kmin, c_pos, c_nn = (rep(a) for a in stats)
        n_vis_g = n_vis[:, lanes]

        def count_ge(t):
            def body(kb, cnts):
                k0 = pl.multiple_of(kb * KB, KB)
                cnts = list(cnts)
                blk = ikey_ref[pl.ds(k0, KB), lanes]
                for r in range(KB // SUBLANES):
                    rows = blk[r * SUBLANES:(r + 1) * SUBLANES]
                    cnts[r % n_acc] = cnts[r % n_acc] + jnp.where(rows >= t, 1, 0)
                return tuple(cnts)
            zero = jnp.zeros((SUBLANES, LANES), I32)
            cnts = lax.fori_loop(0, n_kb, body, (zero,) * n_acc)
            tot = (cnts[0] + cnts[1]) + (cnts[2] + cnts[3])
            for sh in (4, 2, 1):
                tot = tot + pltpu.roll(tot, sh, 0)
            return tot

        pos = c_pos >= topk
        zer = jnp.logical_not(pos) & (c_nn >= topk)
        guess = jnp.maximum(kmax - span, 1)
        c_guess = count_ge(guess)
        ok = c_guess >= topk
        lo_neg = jnp.where(n_vis_g >= topk, kmin, floor)
        lo0 = jnp.where(pos, jnp.where(ok, guess, 1), jnp.where(zer, 0, lo_neg))
        clo0 = jnp.where(pos, jnp.where(ok, c_guess, c_pos), jnp.where(zer, c_nn, n_vis_g))
        hi0 = jnp.where(pos, jnp.where(ok, jnp.where(kmax == INT_MAX, kmax, kmax + 1), guess),
                        jnp.where(zer, 1, 0))

        def unconverged(lo, hi, clo):
            return jnp.max(jnp.where((clo > topk) & (lo + 1 < hi), 1.0, 0.0))

        def bisect_cond(carry):
            it, todo = carry[0], carry[1]
            return (it < 60) & (todo > 0.0)

        def one_pass(it, lo, hi, clo):
            int_mid = jnp.right_shift(lo, 1) + jnp.right_shift(hi, 1) + (lo & hi & 1)
            val_mid = _float_to_key(0.5 * (_key_to_float(lo) + _key_to_float(hi - 1)))
            val_ok = (val_mid > lo) & (val_mid < hi) & (jnp.where(it < 24, 1, 0) > 0)
            mid = jnp.where(val_ok, val_mid, int_mid)
            c = count_ge(mid)
            ge = c >= topk
            return jnp.where(ge, mid, lo), jnp.where(ge, hi, mid), jnp.where(ge, c, clo)

        def bisect_body(carry):
            it, _, lo, hi, clo = carry
            for u in range(PASSES_PER_CHECK):
                lo, hi, clo = one_pass(it + u, lo, hi, clo)
            return it + PASSES_PER_CHECK, unconverged(lo, hi, clo), lo, hi, clo

        _, _, lo, _, clo = lax.while_loop(
            bisect_cond, bisect_body, (jnp.int32(0), unconverged(lo0, hi0, clo0), lo0, hi0, clo0))
        return lo, clo

    found = [search(g) for g in range(tq // LANES)]
    tau = jnp.concatenate([f[0] for f in found], axis=1)
    clo = jnp.concatenate([f[1] for f in found], axis=1)
    tau_row = tau[:1]

    def count_ge_all(t):
        def body(kb, cnt):
            k0 = pl.multiple_of(kb * KB, KB)
            return cnt + jnp.sum(jnp.where(ikey_ref[pl.ds(k0, KB), :] >= t[:1], 1, 0), axis=0, keepdims=True)
        return lax.fori_loop(0, n_kb, body, jnp.zeros((1, tq), I32))

    @pl.when(jnp.max(clo) > topk)
    def _():
        need = (topk - count_ge_all(tau + 1)).astype(F32)
        tri = (lax.broadcasted_iota(I32, (KB, KB), 0)
               >= lax.broadcasted_iota(I32, (KB, KB), 1)).astype(BF16)

        def drop_block(kb, run):
            k0 = pl.multiple_of(kb * KB, KB)
            blk = ikey_ref[pl.ds(k0, KB), :]
            eq = blk == tau_row
            pref = jnp.dot(tri, jnp.where(eq, 1.0, 0.0).astype(BF16), preferred_element_type=F32)
            keep = (run + pref) <= need
            ikey_ref[pl.ds(k0, KB), :] = jnp.where(eq & jnp.logical_not(keep), INT_MIN, blk)
            return run + pref[KB - 1:KB, :]

        lax.fori_loop(0, n_kb, drop_block, jnp.zeros((1, tq), F32))

    neg_bound = -bound_ref[0, 0]
    acc_ref[...] = jnp.zeros(acc_ref.shape, F32)

    def attend_block(kb, carry):
        k0 = pl.multiple_of(kb * KB, KB)
        mb = jnp.where(ikey_ref[pl.ds(k0, KB), :] >= tau_row, neg_bound, NEG)
        logits = [lax.dot_general(ka_ref[0, pl.ds(k0, KB), (h // 2) * LANES:(h // 2 + 1) * LANES],
                                  qpad_ref[h], _NT, preferred_element_type=F32) for h in range(H)]
        probs = [jnp.exp(s + mb).astype(BF16) for s in logits]
        for h in range(H):
            pr = h // 2
            acc_ref[h] += jnp.dot(vt_ref[0, kb, pr * PAIR_ROWS:(pr + 1) * PAIR_ROWS, :], probs[h],
                                  preferred_element_type=F32)
        return carry

    lax.fori_loop(0, n_kb, attend_block, 0)
    for h in range(H):
        l_ref[h] = acc_ref[h, LANES:LANES + 1, :]
    l_min = jnp.min(l_ref[...])

    @pl.when(jnp.logical_not(l_min >= L_FLOOR))
    def _():
        m_ref[...] = jnp.full(m_ref.shape, NEG, F32)
        l_ref[...] = jnp.zeros(l_ref.shape, F32)
        acc_ref[...] = jnp.zeros(acc_ref.shape, F32)

        def attend_exact(kb, carry):
            k0 = pl.multiple_of(kb * KB, KB)
            mb = jnp.where(ikey_ref[pl.ds(k0, KB), :] >= tau_row, 0.0, NEG)
            for h in range(H):
                pr = h // 2
                kblk = ka_ref[0, pl.ds(k0, KB), pr * LANES:(pr + 1) * LANES]
                s = lax.dot_general(kblk, qpad_ref[h], _NT, preferred_element_type=F32) + mb
                m_prev = m_ref[h]
                m_new = jnp.maximum(m_prev, jnp.max(s, axis=0, keepdims=True))
                alpha = jnp.exp(m_prev - m_new)
                p = jnp.exp(s - m_new)
                l_ref[h] = alpha * l_ref[h] + jnp.sum(p, axis=0, keepdims=True)
                pv = jnp.dot(vt_ref[0, kb, pr * PAIR_ROWS:pr * PAIR_ROWS + LANES, :], p.astype(BF16),
                             preferred_element_type=F32)
                acc_ref[h, :LANES, :] = alpha * acc_ref[h, :LANES, :] + pv
                m_ref[h] = m_new
            return carry

        lax.fori_loop(0, n_kb, attend_exact, 0)

    row = lax.broadcasted_iota(I32, (LANES, tq), 0)
    for pr in range(H // 2):
        even = acc_ref[2 * pr, :LANES, :] / l_ref[2 * pr]
        odd = acc_ref[2 * pr + 1, :LANES, :] / l_ref[2 * pr + 1]
        out_t = jnp.where(row < HEAD_DIM, even, odd)
        o_ref[0, :, pr * LANES:(pr + 1) * LANES] = jnp.transpose(out_t).astype(BF16)


def _dsa(bound, qa, qi, kiwi, kidx, ka, vt, *, tq, q_period, s_valid, q_pos0, topk):
    b, t, w = qa.shape
    s_pad = kidx.shape[1]
    kb_size = vt.shape[3]
    assert t % tq == 0 and s_pad == vt.shape[1] * kb_size and tq % LANES == 0
    assert kb_size & (kb_size - 1) == 0 and q_period & (q_period - 1) == 0
    qspec = lambda width: pl.BlockSpec((1, tq, width), lambda bi, ji: (bi, ji, 0))
    kspec = lambda *dims: pl.BlockSpec((1,) + dims, lambda bi, ji: (bi,) + (0,) * len(dims),
                                       pipeline_mode=pl.Buffered(1))
    kern = functools.partial(_dsa_kernel, tq=tq, q_period=q_period, kb_size=kb_size, s_valid=s_valid,
                             q_pos0=q_pos0, topk=topk)
    return pl.pallas_call(
        kern,
        grid=(b, t // tq),
        in_specs=[pl.BlockSpec(memory_space=pltpu.SMEM),
                  qspec(w), qspec(w), qspec(LANES), kspec(s_pad, D_IDX), kspec(s_pad, w),
                  kspec(s_pad // kb_size, VT_ROWS, kb_size)],
        out_specs=qspec(w),
        out_shape=jax.ShapeDtypeStruct((b, t, w), BF16),
        scratch_shapes=[
            pltpu.VMEM((s_pad, tq), I32),
            pltpu.VMEM((N_HEADS * tq, D_IDX), BF16),
            pltpu.VMEM((N_HEADS, tq, LANES), BF16),
            pltpu.VMEM((N_HEADS, kb_size, tq), F32),
            pltpu.VMEM((N_HEADS, 1, tq), F32),
            pltpu.VMEM((N_HEADS, 1, tq), F32),
            pltpu.VMEM((N_HEADS, PAIR_ROWS, tq), F32),
        ],
        compiler_params=pltpu.CompilerParams(
            dimension_semantics=("arbitrary", "arbitrary"), vmem_limit_bytes=VMEM_LIMIT),
        name="dsa",
    )(bound, qa, qi, kiwi, kidx, ka, vt)


def _band_kernel(*refs, tq, kb_size, nkb):
    q_ref = refs[0]
    k_refs = refs[1:1 + nkb]
    v_refs = refs[1 + nkb:1 + 2 * nkb]
    bias_ref = refs[1 + 2 * nkb]
    o_ref = refs[2 + 2 * nkb]
    j = pl.program_id(1)
    q = q_ref[0]
    lane = lax.broadcasted_iota(I32, (tq, LANES), 1)
    for pr in range(N_HEADS // 2):
        p0 = pr * LANES
        qpair = q[:, p0:p0 + LANES]
        outs = []
        for par in range(2):
            h = 2 * pr + par
            own = (lane >= HEAD_DIM) if par else (lane < HEAD_DIM)
            qp = jnp.where(own, qpair, jnp.zeros_like(qpair))
            parts = []
            for i in range(nkb):
                s = lax.dot_general(qp, k_refs[i][0, :, p0:p0 + LANES], _NT, preferred_element_type=F32)
                s = s + bias_ref[h, :, i * kb_size:(i + 1) * kb_size]
                if i < nkb - 1:
                    s = jnp.where(j + (i - (nkb - 1)) >= 0, s, NEG)
                parts.append(s)
            m = parts[0].max(axis=1, keepdims=True)
            for s in parts[1:]:
                m = jnp.maximum(m, s.max(axis=1, keepdims=True))
            l = jnp.zeros((tq, 1), F32)
            o = jnp.zeros((tq, LANES), F32)
            for i in range(nkb):
                p = jnp.exp(parts[i] - m)
                l = l + jnp.sum(p, axis=1, keepdims=True)
                o = o + jnp.dot(p.astype(BF16), v_refs[i][0, :, p0:p0 + LANES], preferred_element_type=F32)
            outs.append(o / l)
        o_ref[0, :, p0:p0 + LANES] = jnp.where(lane < HEAD_DIM, outs[0], outs[1]).astype(BF16)


def _band(q, k, v, bias, *, tq, kb_size, nkb):
    b, t, w = q.shape
    qspec = pl.BlockSpec((1, tq, w), lambda bi, ji: (bi, ji, 0))

    def kspec(i):
        off = i - (nkb - 1)
        return pl.BlockSpec((1, kb_size, w), lambda bi, ji: (bi, jnp.maximum(ji + off, 0), 0))

    kern = functools.partial(_band_kernel, tq=tq, kb_size=kb_size, nkb=nkb)
    return pl.pallas_call(
        kern,
        grid=(b, t // tq),
        in_specs=[qspec] + [kspec(i) for i in range(nkb)] * 2 + [_const_spec(bias.shape)],
        out_specs=qspec,
        out_shape=jax.ShapeDtypeStruct((b, t, w), BF16),
        compiler_params=pltpu.CompilerParams(
            dimension_semantics=("arbitrary", "arbitrary"), vmem_limit_bytes=VMEM_LIMIT),
        name="band",
    )(q, *([k] * nkb), *([v] * nkb), bias)


def _band_bias(table, tq, nk):
    n_rel = table.shape[0]
    pad = 2 * (tq + nk)
    ext = jnp.pad(table[::-1].astype(F32), ((pad, pad), (0, 0)), mode='edge')
    length = tq + nk
    start = REL_CLIP + pad - (BAND_ROWS + tq - 1)
    g = lax.slice_in_dim(ext, start, start + length, axis=0)
    assert start >= 0 and start + length <= n_rel + 2 * pad
    rows = jnp.tile(g, (tq, 1))[:tq * (length - 1)].reshape(tq, length - 1, -1)
    b = rows[:, tq - 1:tq - 1 + nk, :]
    r = jnp.arange(tq)[:, None]
    c = jnp.arange(nk)[None, :]
    qc, kc = r // CHUNK, c // CHUNK
    ok = (kc >= qc) & (kc <= qc + BAND_CHUNKS)
    return jnp.where(ok[None], jnp.moveaxis(b, -1, 0), NEG)


def _ffn_kernel(x_ref, oa_ref, ob_ref, gate_ref, woa_ref, wob_ref, wout_ref, gffn_ref,
                wup_ref, wdn_ref, y_ref, *, ff_chunk):
    d = x_ref.shape[1]
    a = jnp.dot(oa_ref[...], woa_ref[...], preferred_element_type=F32)
    b = jnp.dot(ob_ref[...], wob_ref[...], preferred_element_type=F32)
    m = gate_ref[:, :d] * a + gate_ref[:, d:] * b
    x1 = x_ref[...] + jnp.dot(m.astype(BF16), wout_ref[...], preferred_element_type=F32)
    ms = jnp.mean(x1 * x1, axis=-1, keepdims=True)
    xn = (x1 * lax.rsqrt(ms + EPS) * gffn_ref[...]).astype(BF16)
    acc = x1
    for c in range(wup_ref.shape[1] // ff_chunk):
        h = jnp.dot(xn, wup_ref[:, c * ff_chunk:(c + 1) * ff_chunk], preferred_element_type=F32)
        h = jnp.square(jnp.maximum(h, 0.0))
        acc = acc + jnp.dot(h.astype(BF16), wdn_ref[c * ff_chunk:(c + 1) * ff_chunk, :],
                            preferred_element_type=F32)
    y_ref[...] = acc


def _ffn(x2d, oa, ob, gate, fw, tm):
    n, d = x2d.shape
    w = oa.shape[1]
    row = lambda i: (i, 0)
    kern = functools.partial(_ffn_kernel, ff_chunk=min(1024, fw['wup'].shape[1]))
    return pl.pallas_call(
        kern,
        grid=(n // tm,),
        in_specs=[pl.BlockSpec((tm, d), row), pl.BlockSpec((tm, w), row), pl.BlockSpec((tm, w), row),
                  pl.BlockSpec((tm, 2 * d), row),
                  _const_spec(fw['woa'].shape), _const_spec(fw['wob'].shape), _const_spec(fw['wout'].shape),
                  _const_spec((1, d)), _const_spec(fw['wup'].shape), _const_spec(fw['wdn'].shape)],
        out_specs=pl.BlockSpec((tm, d), row),
        out_shape=jax.ShapeDtypeStruct((n, d), F32),
        compiler_params=pltpu.CompilerParams(
            dimension_semantics=("arbitrary",), vmem_limit_bytes=VMEM_LIMIT),
        name="ffn",
    )(x2d, oa, ob, gate, fw['woa'], fw['wob'], fw['wout'], fw['gffn'], fw['wup'], fw['wdn'])


def _rope_tables(pos):
    half = ROT_DIM // 2
    inv = ROPE_THETA ** (-jnp.arange(half, dtype=F32) / half)
    ang = pos.astype(F32)[:, None] * inv[None, :]
    cos, sin = jnp.cos(ang), jnp.sin(ang)
    t = pos.shape[0]
    rest = HEAD_DIM - ROT_DIM
    c = jnp.concatenate([cos, cos, jnp.ones((t, rest), F32)], axis=1)
    sa = jnp.concatenate([jnp.zeros((t, half), F32), sin, jnp.zeros((t, rest), F32)], axis=1)
    sb = jnp.concatenate([-sin, jnp.zeros((t, half + rest), F32)], axis=1)
    rep = LANES // HEAD_DIM
    return tuple(jnp.tile(a, (1, rep)) for a in (c, sa, sb))


def _tile_gain(g):
    return jnp.tile(g.astype(F32), N_HEADS)[None, :]


def _pick_tile(n, pref):
    t = min(n, pref)
    while n % t:
        t //= 2
    return t


def kernel(x_prompt, x_sample, cache_k_a, cache_v_a, cache_k_idx, cache_k_b, cache_v_b, norm_mix, w_in,
           qnorm_a, knorm_a, knorm_idx, qnorm_b, knorm_b, rel_bias_b, w_o_a, w_o_b, w_out, norm_ffn,
           w_up, w_down):
    B, T, D = x_prompt.shape
    Bs, Ts, _ = x_sample.shape
    depth = w_in.shape[0]
    past = cache_k_a.shape[2]
    rows_b = cache_k_b.shape[2]
    W = W_HEADS
    H = N_HEADS
    assert T % CHUNK == 0 and Ts % CHUNK == 0 and past % CHUNK == 0
    assert rows_b == BAND_ROWS and T >= BAND_ROWS
    topk_p = min(TOPK_MAX, T // 4)
    topk_s = min(TOPK_MAX, (past + Ts) // 4)
    keep_p = min(BAND_ROWS, T)

    pos_p = jnp.arange(T, dtype=jnp.int32)
    pos_s = past + jnp.arange(Ts, dtype=jnp.int32)
    tab_p = _rope_tables(pos_p)
    tab_s = tuple(jnp.tile(t, (Bs, 1)) for t in _rope_tables(pos_s))

    seg = jnp.where((jnp.arange(W)[:, None] // HEAD_DIM) == (jnp.arange(W)[None, :] // HEAD_DIM),
                    1.0 / HEAD_DIM, 0.0).astype(BF16)

    xp = x_prompt.reshape(B * T, D)
    xs = x_sample.reshape(Bs * Ts, D)
    outs_p = [[] for _ in range(5)]
    outs_s = [[] for _ in range(5)]

    tm_p = _pick_tile(T, 512)
    tq_p = _pick_tile(T, 256)
    band_tq = _pick_tile(T, 256)
    assert BAND_ROWS % band_tq == 0 and tq_p % LANES == 0

    for l in range(depth):
        wl = w_in[l].astype(BF16)
        o_ki = 4 * W
        o_b = o_ki + D_IDX + H
        o_g = o_b + 3 * W
        w2 = jnp.pad(wl[:, o_ki:o_b], ((0, 0), (0, LANES - D_IDX - H)))
        pw = dict(
            gmix=norm_mix[l][None, :].astype(F32),
            w1=wl[:, :o_ki], w2=w2, w3=wl[:, o_b:o_g], w4=wl[:, o_g:], wvt=wl[:, 2 * W:3 * W].T, seg=seg,
            gqa=_tile_gain(qnorm_a[l]), gka=_tile_gain(knorm_a[l]),
            gki=jnp.pad(knorm_idx[l].astype(F32), (0, LANES - D_IDX))[None, :],
            gqb=_tile_gain(qnorm_b[l]), gkb=_tile_gain(knorm_b[l]),
        )
        fw = dict(woa=w_o_a[l].astype(BF16), wob=w_o_b[l].astype(BF16), wout=w_out[l].astype(BF16),
                  gffn=norm_ffn[l][None, :].astype(F32), wup=w_up[l].astype(BF16),
                  wdn=w_down[l].astype(BF16))
        table = rel_bias_b[l]

        pp = _project(xp, tab_p, pw, tm_p)
        r3 = lambda a: a.reshape(B, T, a.shape[-1])
        bound_a = (1.01 * HEAD_DIM * ATTN_SCALE * jnp.max(jnp.abs(qnorm_a[l])) * jnp.max(jnp.abs(knorm_a[l]))
                   ).astype(F32).reshape(1, 1)
        oa = _dsa(bound_a, r3(pp['qa']), r3(pp['qi']), r3(pp['kiwi']), r3(pp['ki_bf']), r3(pp['ka_bf']),
                  pp['va_t'].reshape(B, T // tm_p, VT_ROWS, tm_p),
                  tq=tq_p, q_period=tq_p, s_valid=T, q_pos0=0, topk=topk_p)
        nkb = BAND_ROWS // band_tq + 1
        ob = _band(r3(pp['qb']), r3(pp['kb_bf']), r3(pp['vb_bf']),
                   _band_bias(table, band_tq, nkb * band_tq), tq=band_tq, kb_size=band_tq, nkb=nkb)
        xp = _ffn(xp, oa.reshape(B * T, W), ob.reshape(B * T, W), pp['gate'], fw, _pick_tile(B * T, 256))
        outs_p[0].append(pp['ka'].reshape(B, T, H, HEAD_DIM))
        outs_p[1].append(pp['va'].reshape(B, T, H, HEAD_DIM))
        outs_p[2].append(r3(pp['kiwi'])[:, :, :D_IDX])
        outs_p[3].append(pp['kb'].reshape(B, T, H, HEAD_DIM)[:, T - keep_p:])
        outs_p[4].append(pp['vb'].reshape(B, T, H, HEAD_DIM)[:, T - keep_p:])

        ps = _project(xs, tab_s, pw, _pick_tile(Bs * Ts, 256))
        s3 = lambda a: a.reshape(Bs, Ts, a.shape[-1])
        s_all = past + Ts
        kb_s = LANES
        s_pad = -(-s_all // kb_s) * kb_s
        tq_s = -(-Ts // LANES) * LANES
        assert tq_s % Ts == 0 and Ts & (Ts - 1) == 0

        def with_cache(cache, new, width):
            full = jnp.concatenate([cache.reshape(Bs, -1, width).astype(BF16), s3(new)], axis=1)
            return jnp.pad(full, ((0, 0), (0, s_pad - s_all), (0, 0)))

        rep_q = lambda a: jnp.tile(s3(a), (1, tq_s // Ts, 1))
        v_all = with_cache(cache_v_a[l], ps['va'].astype(BF16), W)
        n_kb_s = s_pad // kb_s
        v_t = v_all.reshape(Bs, n_kb_s, kb_s, H // 2, LANES).transpose(0, 1, 3, 4, 2)
        v_t = jnp.concatenate([v_t, jnp.ones((Bs, n_kb_s, H // 2, ONES_ROWS, kb_s), BF16)], axis=3)
        v_t = v_t.reshape(Bs, n_kb_s, VT_ROWS, kb_s)
        cache_norm = jnp.sqrt(jnp.max(jnp.sum(jnp.square(cache_k_a[l].astype(F32)), axis=-1)))
        bound_s = jnp.maximum(bound_a, (1.01 * HEAD_DIM ** 0.5 * ATTN_SCALE) * jnp.max(jnp.abs(qnorm_a[l]))
                              * cache_norm).astype(F32).reshape(1, 1)
        oa = _dsa(bound_s, rep_q(ps['qa']), rep_q(ps['qi']), rep_q(ps['kiwi']),
                  with_cache(cache_k_idx[l], ps['ki_bf'], D_IDX),
                  with_cache(cache_k_a[l], ps['ka_bf'], W), v_t,
                  tq=tq_s, q_period=Ts, s_valid=s_all, q_pos0=past, topk=topk_s)[:, :Ts]
        kb_all = jnp.concatenate([cache_k_b[l].reshape(Bs, rows_b, W).astype(BF16), s3(ps['kb_bf'])], axis=1)
        vb_all = jnp.concatenate([cache_v_b[l].reshape(Bs, rows_b, W).astype(BF16), s3(ps['vb_bf'])], axis=1)
        ob = _band(s3(ps['qb']), kb_all, vb_all, _band_bias(table, Ts, rows_b + Ts),
                   tq=Ts, kb_size=rows_b + Ts, nkb=1)
        xs = _ffn(xs, oa.reshape(Bs * Ts, W), ob.reshape(Bs * Ts, W), ps['gate'], fw,
                  _pick_tile(Bs * Ts, 256))
        outs_s[0].append(ps['ka'].reshape(Bs, Ts, H, HEAD_DIM))
        outs_s[1].append(ps['va'].reshape(Bs, Ts, H, HEAD_DIM))
        outs_s[2].append(s3(ps['kiwi'])[:, :, :D_IDX])
        outs_s[3].append(ps['kb'].reshape(Bs, Ts, H, HEAD_DIM))
        outs_s[4].append(ps['vb'].reshape(Bs, Ts, H, HEAD_DIM))

    return (xp.reshape(B, T, D), xs.reshape(Bs, Ts, D),
            *[jnp.stack(o) for o in outs_p], *[jnp.stack(o) for o in outs_s])
```
